```python
import math
import jax, jax.numpy as jnp
from jax import lax
import numpy as np

D_MODEL = 4096
BATCH = 2
SEQ = 8192
DEPTH = 2

GRID_W = 64
CTX_LEN = 256
N_MIXERS = 2
ATTN_HEADS = 32
ATTN_KV_HEADS = 8
ATTN_HEAD_DIM = D_MODEL // ATTN_HEADS
ATTN_WINDOW = 128
ATTN_BLOCK = 128
ROPE_BASE = 10000.0
RET_HEADS = 16
RET_KEY_DIM = D_MODEL // RET_HEADS
RET_VALUE_DIM = 2 * RET_KEY_DIM
RET_CHUNK = 128
N_EXPERTS = 32
TOP_K = 4
EXPERT_DIM = 512
SWIGLU_LIMIT = 7.0
SWIGLU_ALPHA = 1.702
NORM_EPS = 1e-6

kernel_name = "hybrid_swa_retention_moe_dit_block"

F32 = jnp.float32


def rms_norm(x, w):
    xf = x.astype(F32)
    y = xf * lax.rsqrt(jnp.mean(xf * xf, axis=-1, keepdims=True) + NORM_EPS)
    return (y * w.astype(F32)).astype(x.dtype)


def _rotate(x, ang):
    d2 = x.shape[-1] // 2
    shape = (1, ang.shape[0]) + (1,) * (x.ndim - 3) + (d2,)
    cos = jnp.cos(ang).reshape(shape).astype(x.dtype)
    sin = jnp.sin(ang).reshape(shape).astype(x.dtype)
    x1, x2 = x[..., :d2], x[..., d2:]
    return jnp.concatenate([x1 * cos - x2 * sin, x1 * sin + x2 * cos], axis=-1)


def axial_rope(x, rows, cols):
    half = x.shape[-1] // 2
    inv = 1.0 / (ROPE_BASE ** (jnp.arange(0, half, 2, dtype=F32) / half))
    xr = _rotate(x[..., :half], rows.astype(F32)[:, None] * inv[None, :])
    xc = _rotate(x[..., half:], cols.astype(F32)[:, None] * inv[None, :])
    return jnp.concatenate([xr, xc], axis=-1)


def windowed_sink_attention(h_lat, h_ctx, w_qkv, w_o, sinks, with_ctx_out):
    B, S, _ = h_lat.shape
    C = h_ctx.shape[1]
    H, KV, hd, L = ATTN_HEADS, ATTN_KV_HEADS, ATTN_HEAD_DIM, ATTN_BLOCK
    G = H // KV
    nq = H * hd
    scale = hd ** -0.5
    qkv = h_lat @ w_qkv
    q = qkv[..., :nq].reshape(B, S, KV, G, hd)
    k = qkv[..., nq:nq + KV * hd].reshape(B, S, KV, hd)
    v = qkv[..., nq + KV * hd:].reshape(B, S, KV, hd)
    t = jnp.arange(S)
    rows, cols = t // GRID_W, t % GRID_W
    q = axial_rope(q, rows, cols)
    k = axial_rope(k, rows, cols)
    kv_c = h_ctx @ w_qkv[:, nq:]
    kc = kv_c[..., :KV * hd].reshape(B, C, KV, hd)
    vc = kv_c[..., KV * hd:].reshape(B, C, KV, hd)
    sink = sinks.astype(F32).reshape(KV, G)

    nb = S // L
    qb = q.reshape(B, nb, L, KV, G, hd)

    def neighbours(a):
        ap = jnp.pad(a, ((0, 0), (L, L), (0, 0), (0, 0))).reshape(B, nb + 2, L, KV, hd)
        return jnp.concatenate([ap[:, :-2], ap[:, 1:-1], ap[:, 2:]], axis=2)

    kb, vb = neighbours(k), neighbours(v)
    qi = jnp.arange(L)[:, None]
    kj = jnp.arange(3 * L)[None, :]
    blk = jnp.arange(nb)[:, None, None]
    kpos = (blk - 1) * L + kj[None]
    mask = (jnp.abs(qi + L - kj)[None] <= ATTN_WINDOW) & (kpos >= 0) & (kpos < S)
    s_loc = jnp.einsum('bnqhgd,bnkhd->bhgnqk', qb, kb).astype(F32) * scale
    s_loc = jnp.where(mask, s_loc, -jnp.inf)
    s_ctx = jnp.einsum('bnqhgd,bchd->bhgnqc', qb, kc).astype(F32) * scale
    s_snk = jnp.broadcast_to(sink[None, :, :, None, None, None], s_loc.shape[:-1] + (1,))
    p = jax.nn.softmax(jnp.concatenate([s_loc, s_ctx, s_snk], axis=-1), axis=-1)
    p_loc = p[..., :3 * L].astype(v.dtype)
    p_ctx = p[..., 3 * L:3 * L + C].astype(v.dtype)
    o = (jnp.einsum('bhgnqk,bnkhd->bnqhgd', p_loc, vb)
         + jnp.einsum('bhgnqc,bchd->bnqhgd', p_ctx, vc))
    o_lat = o.reshape(B, S, nq) @ w_o

    o_ctx = None
    if with_ctx_out:
        qc = (h_ctx @ w_qkv[:, :nq]).reshape(B, C, KV, G, hd)
        s_cc = jnp.einsum('bqhgd,bkhd->bhgqk', qc, kc).astype(F32) * scale
        s_cs = jnp.broadcast_to(sink[None, :, :, None, None], s_cc.shape[:-1] + (1,))
        p_cc = jax.nn.softmax(jnp.concatenate([s_cc, s_cs], axis=-1), axis=-1)[..., :C]
        o_c = jnp.einsum('bhgqk,bkhd->bqhgd', p_cc.astype(vc.dtype), vc)
        o_ctx = o_c.reshape(B, C, nq) @ w_o
    return o_lat, o_ctx


def _decay_terms(log_gamma, L):
    idx = jnp.arange(L, dtype=F32)
    diff = idx[:, None] - idx[None, :]
    expo = jnp.where(diff[None] >= 0, diff[None] * log_gamma[:, None, None], -jnp.inf)
    d_mask = jnp.exp(expo)
    xi = jnp.exp((idx[None, :] + 1.0) * log_gamma[:, None])
    zeta = jnp.exp((L - 1.0 - idx[None, :]) * log_gamma[:, None])
    chunk_decay = jnp.exp(L * log_gamma)
    return d_mask, xi, zeta, chunk_decay


def retention_chunkwise(q, k, v, log_gamma, s0):
    B, T, H, dk = q.shape
    dv = v.shape[-1]
    L = RET_CHUNK
    n = T // L
    qc = q.astype(F32).reshape(B, n, L, H, dk)
    kc = k.astype(F32).reshape(B, n, L, H, dk)
    vc = v.astype(F32).reshape(B, n, L, H, dv)
    d_mask, xi, zeta, chunk_decay = _decay_terms(log_gamma, L)
    inner = jnp.einsum('bnihd,bnmhd->bnhim', qc, kc) * d_mask[None, None]
    o_inner = jnp.einsum('bnhim,bnmhe->bnihe', inner, vc)
    kv = jnp.einsum('bnmhd,hm,bnmhe->bnhde', kc, zeta, vc)

    def step(s, kv_j):
        return chunk_decay[None, :, None, None] * s + kv_j, s

    s_final, s_prev = lax.scan(step, s0.astype(F32), jnp.moveaxis(kv, 1, 0))
    s_prev = jnp.moveaxis(s_prev, 0, 1)
    o_cross = jnp.einsum('bnihd,bnhde->bnihe', qc, s_prev) * xi.T[None, None, :, :, None]
    return (o_inner + o_cross).reshape(B, T, H, dv), s_final


def retention_final_state(k, v, log_gamma):
    T = k.shape[1]
    w = jnp.exp((T - 1.0 - jnp.arange(T, dtype=F32))[:, None] * log_gamma[None, :])
    return jnp.einsum('bthd,th,bthe->bhde', k.astype(F32), w, v.astype(F32))


def retention_mixer(h_lat, h_ctx, w_in, w_out, a_fwd, a_bwd, with_ctx_out):
    B, S, _ = h_lat.shape
    C = h_ctx.shape[1]
    H, dk, dv = RET_HEADS, RET_KEY_DIM, RET_VALUE_DIM
    qd, vd = H * dk, H * dv
    k_scale = dk ** -0.5
    lg_f = jnp.log1p(-jnp.exp(a_fwd.astype(F32)))
    lg_b = jnp.log1p(-jnp.exp(a_bwd.astype(F32)))

    def flip(a):
        return a[:, ::-1]

    def head_out(o, g):
        o = o * lax.rsqrt(jnp.mean(o * o, axis=-1, keepdims=True) + NORM_EPS)
        return (jax.nn.silu(g) * o.reshape(g.shape).astype(g.dtype)) @ w_out

    o_ctx = None
    if with_ctx_out:
        pc = h_ctx @ w_in
        qc = pc[..., :qd].reshape(B, C, H, dk)
        kc = pc[..., qd:2 * qd].reshape(B, C, H, dk) * k_scale
        vc = pc[..., 2 * qd:2 * qd + vd].reshape(B, C, H, dv)
        gc = pc[..., 2 * qd + vd:]
        zeros = jnp.zeros((B, H, dk, dv), F32)
        oc_f, sc_f = retention_chunkwise(qc, kc, vc, lg_f, zeros)
        oc_b, sc_b = retention_chunkwise(flip(qc), flip(kc), flip(vc), lg_b, zeros)
        o_ctx = head_out(oc_f + flip(oc_b), gc)
    else:
        pc = h_ctx @ w_in[:, qd:2 * qd + vd]
        kc = pc[..., :qd].reshape(B, C, H, dk) * k_scale
        vc = pc[..., qd:].reshape(B, C, H, dv)
        sc_f = retention_final_state(kc, vc, lg_f)
        sc_b = retention_final_state(flip(kc), flip(vc), lg_b)

    p = h_lat @ w_in
    q = p[..., :qd].reshape(B, S, H, dk)
    k = p[..., qd:2 * qd].reshape(B, S, H, dk) * k_scale
    v = p[..., 2 * qd:2 * qd + vd].reshape(B, S, H, dv)
    g = p[..., 2 * qd + vd:]
    theta = 1.0 / (ROPE_BASE ** jnp.linspace(0.0, 1.0, dk // 2, dtype=F32))
    ang = jnp.arange(S, dtype=F32)[:, None] * theta[None, :]
    q = _rotate(q, ang)
    k = _rotate(k, ang)
    o_f, _ = retention_chunkwise(q, k, v, lg_f, sc_f)
    o_b, _ = retention_chunkwise(flip(q), flip(k), flip(v), lg_b, sc_b)
    o_lat = head_out(o_f + flip(o_b), g)
    return o_lat, o_ctx


def moe_ffn(t, router_w, router_b, w1, b1, w2, b2):
    logits = (t @ router_w + router_b).astype(F32)
    top_val, top_idx = lax.top_k(logits, TOP_K)
    top_w = jax.nn.softmax(top_val, axis=-1)
    gates = jnp.sum(jax.nn.one_hot(top_idx, N_EXPERTS, dtype=F32) * top_w[..., None], axis=1)
    gates = gates.astype(t.dtype)
    out = jnp.zeros_like(t)
    for e in range(N_EXPERTS):
        gu = t @ w1[e] + b1[e]
        gate = jnp.minimum(gu[..., :EXPERT_DIM], SWIGLU_LIMIT)
        up = jnp.clip(gu[..., EXPERT_DIM:], -SWIGLU_LIMIT, SWIGLU_LIMIT)
        act = (up + 1.0) * (gate * jax.nn.sigmoid(SWIGLU_ALPHA * gate))
        out = out + gates[:, e:e + 1] * (act @ w2[e] + b2[e])
    return out


def setup_inputs(seed: int = 0) -> dict:
    key = jax.random.key(seed)
    ks = jax.random.split(key, 22)
    D = D_MODEL
    n_attn = (DEPTH + N_MIXERS - 1) // N_MIXERS
    n_ret = DEPTH // N_MIXERS
    qkv_cols = (ATTN_HEADS + 2 * ATTN_KV_HEADS) * ATTN_HEAD_DIM
    ret_cols = 2 * RET_HEADS * RET_KEY_DIM + 2 * RET_HEADS * RET_VALUE_DIM
    base_a = -(5.0 + jnp.arange(RET_HEADS, dtype=F32)) * math.log(2.0)

    def nrm(k, shape, s):
        return jax.random.normal(k, shape, F32) * s

    return {
        "x": nrm(ks[0], (BATCH, SEQ, D), 1.0),
        "c": nrm(ks[1], (BATCH, D), 1.0),
        "ctx": nrm(ks[2], (BATCH, CTX_LEN, D), 1.0),
        "c_ctx": nrm(ks[3], (D,), 1.0),
        "mod_w": nrm(ks[4], (DEPTH, D, 6 * D), 0.5 * D ** -0.5),
        "mod_b": nrm(ks[5], (DEPTH, 6 * D), 0.01),
        "norm1_w": 1.0 + nrm(ks[6], (DEPTH, D), 0.01),
        "norm2_w": 1.0 + nrm(ks[7], (DEPTH, D), 0.01),
        "attn_w_qkv": nrm(ks[8], (n_attn, D, qkv_cols), D ** -0.5),
        "attn_w_o": nrm(ks[9], (n_attn, ATTN_HEADS * ATTN_HEAD_DIM, D), (ATTN_HEADS * ATTN_HEAD_DIM) ** -0.5),
        "attn_sinks": nrm(ks[10], (n_attn, ATTN_HEADS), 0.5),
        "ret_w_in": nrm(ks[11], (n_ret, D, ret_cols), D ** -0.5),
        "ret_w_out": nrm(ks[12], (n_ret, RET_HEADS * RET_VALUE_DIM, D), (RET_HEADS * RET_VALUE_DIM) ** -0.5),
        "ret_a_fwd": base_a + nrm(ks[13], (n_ret, RET_HEADS), 0.1),
        "ret_a_bwd": base_a + nrm(ks[14], (n_ret, RET_HEADS), 0.1),
        "router_w": nrm(ks[15], (DEPTH, D, N_EXPERTS), D ** -0.5),
        "router_b": nrm(ks[16], (DEPTH, N_EXPERTS), 0.01),
        "exp_w1": nrm(ks[17], (DEPTH, N_EXPERTS, D, 2 * EXPERT_DIM), D ** -0.5),
        "exp_b1": nrm(ks[18], (DEPTH, N_EXPERTS, 2 * EXPERT_DIM), 0.01),
        "exp_w2": nrm(ks[19], (DEPTH, N_EXPERTS, EXPERT_DIM, D), EXPERT_DIM ** -0.5),
        "exp_b2": nrm(ks[20], (DEPTH, N_EXPERTS, D), 0.01),
        "final_norm_w": 1.0 + nrm(ks[21], (D,), 0.01),
    }


def reference(x, c, ctx, c_ctx, mod_w, mod_b, norm1_w, norm2_w, attn_w_qkv, attn_w_o, attn_sinks,
              ret_w_in, ret_w_out, ret_a_fwd, ret_a_bwd, router_w, router_b,
              exp_w1, exp_b1, exp_w2, exp_b2, final_norm_w):
    B, S, D = x.shape
    C = ctx.shape[1]
    h_lat, h_ctx = x, ctx
    s_c = jax.nn.silu(c)
    s_cc = jax.nn.silu(c_ctx)
    for i in range(DEPTH):
        last = i == DEPTH - 1
        slot = i // N_MIXERS
        m_lat = s_c @ mod_w[i] + mod_b[i]
        m_ctx = s_cc @ mod_w[i] + mod_b[i]
        sh1, sc1, g1, sh2, sc2, g2 = jnp.split(m_lat[:, None, :], 6, axis=-1)
        csh1, csc1, cg1, csh2, csc2, cg2 = jnp.split(m_ctx, 6, axis=-1)

        n_lat = rms_norm(h_lat, norm1_w[i]) * (1.0 + sc1) + sh1
        n_ctx = rms_norm(h_ctx, norm1_w[i]) * (1.0 + csc1) + csh1
        if i % N_MIXERS == 0:
            o_lat, o_ctx = windowed_sink_attention(n_lat, n_ctx, attn_w_qkv[slot], attn_w_o[slot],
                                                   attn_sinks[slot], not last)
        else:
            o_lat, o_ctx = retention_mixer(n_lat, n_ctx, ret_w_in[slot], ret_w_out[slot],
                                           ret_a_fwd[slot], ret_a_bwd[slot], not last)
        h_lat = h_lat + g1 * o_lat

        n_lat = rms_norm(h_lat, norm2_w[i]) * (1.0 + sc2) + sh2
        if not last:
            h_ctx = h_ctx + cg1 * o_ctx
            n_ctx = rms_norm(h_ctx, norm2_w[i]) * (1.0 + csc2) + csh2
            tok = jnp.concatenate([n_ctx, n_lat], axis=1).reshape(B * (C + S), D)
            y = moe_ffn(tok, router_w[i], router_b[i], exp_w1[i], exp_b1[i], exp_w2[i], exp_b2[i])
            y = y.reshape(B, C + S, D)
            h_ctx = h_ctx + cg2 * y[:, :C]
            y_lat = y[:, C:]
        else:
            y_lat = moe_ffn(n_lat.reshape(B * S, D), router_w[i], router_b[i], exp_w1[i], exp_b1[i],
                            exp_w2[i], exp_b2[i]).reshape(B, S, D)
        h_lat = h_lat + g2 * y_lat
    return rms_norm(h_lat, final_norm_w)
```

```python
import functools
import math

import jax
import jax.numpy as jnp
from jax import lax
from jax.experimental import pallas as pl
from jax.experimental.pallas import tpu as pltpu

F32 = jnp.float32
BF16 = jnp.bfloat16
I32 = jnp.int32

GRID_W = 64
ATTN_WINDOW = 128
ATTN_BLOCK = 128
RET_CHUNK = 128
ROPE_BASE = 10000.0
TOP_K = 4
SWIGLU_LIMIT = 7.0
SWIGLU_ALPHA = 1.702
NORM_EPS = 1e-6

LANES = 128
NORM_ROWS = 16
VMEM_LIMIT = 56 * 1024 * 1024

_NT = (((1,), (1,)), ((), ()))
_TN = (((0,), (0,)), ((), ()))


def _params(*sem):
    return pltpu.CompilerParams(dimension_semantics=sem, vmem_limit_bytes=VMEM_LIMIT)


def _tile(dim, pref, *also):
    t = min(dim, pref)
    while any(e % t for e in (dim,) + also):
        t //= 2
    assert t >= 8, (dim, pref, also)
    return t


def _mod_kernel(s_ref, w_ref, b_ref, o_ref):
    s = s_ref[...]
    s = s * jax.nn.sigmoid(s)
    o_ref[...] = jnp.dot(s.astype(BF16), w_ref[...].astype(BF16), preferred_element_type=F32) + b_ref[...]


def _modulation(cvec, mod_w, mod_b):
    nl, d, n = mod_w.shape
    r = cvec.shape[0]
    tn = _tile(n, 512)
    return pl.pallas_call(
        _mod_kernel,
        grid=(nl, n // tn),
        in_specs=[
            pl.BlockSpec((r, d), lambda l, j: (0, 0)),
            pl.BlockSpec((None, d, tn), lambda l, j: (l, 0, j)),
            pl.BlockSpec((None, 1, tn), lambda l, j: (l, 0, j)),
        ],
        out_specs=pl.BlockSpec((None, r, tn), lambda l, j: (l, 0, j)),
        out_shape=jax.ShapeDtypeStruct((nl, r, n), F32),
        compiler_params=_params("arbitrary", "arbitrary"),
        name="modulation",
    )(cvec, mod_w, mod_b.reshape(nl, 1, n))


def _norm_rows(x, wm, sh, d):
    ms = jnp.sum(x * x, axis=-1, keepdims=True) * (1.0 / d)
    return x * lax.rsqrt(ms + NORM_EPS) * wm + sh


def _nmm_kernel(*refs, d, tm, tn, col_off, rot, rot_cols, k_lo, k_hi, k_scale):
    if rot is None:
        x_ref, wm_ref, sh_ref, w_ref, o_ref, n_scr = refs
        cos_ref = sin_ref = None
    else:
        x_ref, wm_ref, sh_ref, w_ref, cos_ref, sin_ref, o_ref, n_scr = refs
    j = pl.program_id(1)

    @pl.when(j == 0)
    def _():
        wm = wm_ref[...]
        sh = sh_ref[...]

        def body(r, c):
            rows = pl.ds(pl.multiple_of(r * NORM_ROWS, NORM_ROWS), NORM_ROWS)
            n_scr[rows, :] = _norm_rows(x_ref[rows, :], wm, sh, d).astype(BF16)
            return c

        lax.fori_loop(0, tm // NORM_ROWS, body, 0)

    acc = jnp.dot(n_scr[...], w_ref[...], preferred_element_type=F32)
    col0 = j * tn + col_off
    if k_scale is not None:
        acc = acc * jnp.where((col0 >= k_lo) & (col0 < k_hi), k_scale, 1.0).astype(F32)

    if rot is None:
        o_ref[...] = acc.astype(o_ref.dtype)
        return

    @pl.when(col0 >= rot_cols)
    def _():
        o_ref[...] = acc.astype(o_ref.dtype)

    @pl.when(col0 < rot_cols)
    def _():
        cos = cos_ref[...]
        sin = sin_ref[...]
        if rot == "axial":
            lane = lax.broadcasted_iota(I32, (tm, LANES), 1)
            first = (lane & 32) == 0
            for hh in range(tn // LANES):
                a = acc[:, hh * LANES:(hh + 1) * LANES]
                partner = jnp.where(first, pltpu.roll(a, LANES - 32, 1), pltpu.roll(a, 32, 1))
                o_ref[:, hh * LANES:(hh + 1) * LANES] = (a * cos + partner * sin).astype(o_ref.dtype)
        else:
            for hh in range(tn // (2 * LANES)):
                c0 = hh * 2 * LANES
                x1 = acc[:, c0:c0 + LANES]
                x2 = acc[:, c0 + LANES:c0 + 2 * LANES]
                o_ref[:, c0:c0 + LANES] = (x1 * cos - x2 * sin).astype(o_ref.dtype)
                o_ref[:, c0 + LANES:c0 + 2 * LANES] = (x1 * sin + x2 * cos).astype(o_ref.dtype)


def _norm_matmul(h, wm, sh, w, *, group_rows, out_dtype, col_off=0, n_out=None, rot=None, tabs=None,
                 rot_cols=0, k_lo=0, k_hi=0, k_scale=None, name="norm_matmul"):
    m, d = h.shape
    n_out = w.shape[1] - col_off if n_out is None else n_out
    tm = _tile(m, 512)
    tn = _tile(n_out, 512, col_off, rot_cols, k_lo, k_hi)
    assert group_rows % tm == 0
    in_specs = [
        pl.BlockSpec((tm, d), lambda i, j: (i, 0)),
        pl.BlockSpec((None, 1, d), lambda i, j: ((i * tm) // group_rows, 0, 0)),
        pl.BlockSpec((None, 1, d), lambda i, j: ((i * tm) // group_rows, 0, 0)),
        pl.BlockSpec((d, tn), lambda i, j: (0, j + col_off // tn)),
    ]
    args = [h, wm, sh, w]
    if rot is not None:
        cos, sin = tabs
        s_pos = cos.shape[0]
        assert s_pos % tm == 0 and rot_cols % tn == 0
        nper = s_pos // tm
        in_specs += [pl.BlockSpec((tm, LANES), lambda i, j: (i % nper, 0))] * 2
        args += [cos, sin]
    kern = functools.partial(_nmm_kernel, d=d, tm=tm, tn=tn, col_off=col_off, rot=rot, rot_cols=rot_cols,
                             k_lo=k_lo, k_hi=k_hi, k_scale=k_scale)
    return pl.pallas_call(
        kern,
        grid=(m // tm, n_out // tn),
        in_specs=in_specs,
        out_specs=pl.BlockSpec((tm, tn), lambda i, j: (i, j)),
        out_shape=jax.ShapeDtypeStruct((m, n_out), out_dtype),
        scratch_shapes=[pltpu.VMEM((tm, d), BF16)],
        compiler_params=_params("arbitrary", "arbitrary"),
        name=name,
    )(*args)


def _mmres_kernel(a_ref, w_ref, h_ref, g_ref, o_ref):
    acc = jnp.dot(a_ref[...], w_ref[...], preferred_element_type=F32)
    o_ref[...] = h_ref[...] + g_ref[...] * acc


def _matmul_residual(a, w, h, gate, *, group_rows, name="matmul_residual"):
    m, k = a.shape
    n = w.shape[1]
    tm = _tile(m, 512)
    tn = _tile(n, 512)
    assert group_rows % tm == 0
    return pl.pallas_call(
        _mmres_kernel,
        grid=(m // tm, n // tn),
        in_specs=[
            pl.BlockSpec((tm, k), lambda i, j: (i, 0)),
            pl.BlockSpec((k, tn), lambda i, j: (0, j)),
            pl.BlockSpec((tm, tn), lambda i, j: (i, j)),
            pl.BlockSpec((None, 1, tn), lambda i, j: ((i * tm) // group_rows, 0, j)),
        ],
        out_specs=pl.BlockSpec((tm, tn), lambda i, j: (i, j)),
        out_shape=jax.ShapeDtypeStruct((m, n), F32),
        compiler_params=_params("arbitrary", "arbitrary"),
        name=name,
    )(a, w, h, gate)


def _softmax_pv(s_parts, v_parts, sink_col):
    m = sink_col
    for s in s_parts:
        m = jnp.maximum(m, jnp.max(s, axis=-1, keepdims=True))
    es = [jnp.exp(s - m) for s in s_parts]
    den = jnp.exp(sink_col - m)
    for e in es:
        den = den + jnp.sum(e, axis=-1, keepdims=True)
    inv = 1.0 / den
    o = None
    for e, v in zip(es, v_parts):
        t = jnp.dot((e * inv).astype(BF16), v, preferred_element_type=F32)
        o = t if o is None else o + t
    return o


def _sink_column(sink_ref, kv, g_heads, rows):
    return jnp.concatenate([jnp.full((rows, 1), sink_ref[kv * g_heads + g], F32) for g in range(g_heads)], axis=0)


def _attn_lat_kernel(sink_ref, q_ref, k_ref, v_ref, kc_ref, vc_ref, o_ref, *, s_len, tq, g_heads, hd, scale):
    kv = pl.program_id(1)
    qi = pl.program_id(2)
    blk = ATTN_BLOCK
    win = 3 * blk
    nblk = tq // blk
    kc = kc_ref[...]
    vc = vc_ref[...]
    sink_col = _sink_column(sink_ref, kv, g_heads, blk)
    rows = lax.broadcasted_iota(I32, (g_heads * blk, win), 0) & (blk - 1)
    cols = lax.broadcasted_iota(I32, (g_heads * blk, win), 1)
    rel = rows - cols + ATTN_WINDOW

    def body(bi, c):
        n = qi * nblk + bi
        start = pl.multiple_of(jnp.clip((n - 1) * blk, 0, s_len - win), blk)
        delta = n * blk - start
        r0 = pl.multiple_of(bi * blk, blk)
        q4 = jnp.concatenate([q_ref[pl.ds(r0, blk), g * hd:(g + 1) * hd] for g in range(g_heads)], axis=0)
        kw = k_ref[pl.ds(start, win), :]
        vw = v_ref[pl.ds(start, win), :]
        s_loc = lax.dot_general(q4, kw, _NT, preferred_element_type=F32) * scale
        dist = rel + delta
        s_loc = jnp.where((dist >= 0) & (dist <= 2 * ATTN_WINDOW), s_loc, -jnp.inf)
        s_ctx = lax.dot_general(q4, kc, _NT, preferred_element_type=F32) * scale
        o = _softmax_pv([s_loc, s_ctx], [vw, vc], sink_col)
        for g in range(g_heads):
            o_ref[pl.ds(r0, blk), g * hd:(g + 1) * hd] = o[g * blk:(g + 1) * blk].astype(o_ref.dtype)
        return c

    lax.fori_loop(0, nblk, body, 0)


def _attn_ctx_kernel(sink_ref, q_ref, kc_ref, vc_ref, o_ref, *, c_len, g_heads, hd, scale):
    kv = pl.program_id(1)
    sink_col = _sink_column(sink_ref, kv, g_heads, c_len)
    q4 = jnp.concatenate([q_ref[:, g * hd:(g + 1) * hd] for g in range(g_heads)], axis=0)
    kc = kc_ref[...]
    s = lax.dot_general(q4, kc, _NT, preferred_element_type=F32) * scale
    o = _softmax_pv([s], [vc_ref[...]], sink_col)
    for g in range(g_heads):
        o_ref[:, g * hd:(g + 1) * hd] = o[g * c_len:(g + 1) * c_len].astype(o_ref.dtype)


def _attention(qkv_lat, qkv_ctx, sinks, *, batch, s_len, c_len, n_heads, n_kv, hd):
    g_heads = n_heads // n_kv
    nq = n_heads * hd
    scale = hd ** -0.5
    tq = _tile(s_len, 1024)
    nqt = s_len // tq
    qw = g_heads * hd
    kcol = nq // hd
    vcol = kcol + n_kv
    smem = pl.BlockSpec(memory_space=pltpu.SMEM)
    lat = pl.pallas_call(
        functools.partial(_attn_lat_kernel, s_len=s_len, tq=tq, g_heads=g_heads, hd=hd, scale=scale),
        grid=(batch, n_kv, nqt),
        in_specs=[
            smem,
            pl.BlockSpec((tq, qw), lambda b, h, i: (b * nqt + i, h)),
            pl.BlockSpec((s_len, hd), lambda b, h, i: (b, kcol + h)),
            pl.BlockSpec((s_len, hd), lambda b, h, i: (b, vcol + h)),
            pl.BlockSpec((c_len, hd), lambda b, h, i: (b, kcol + h)),
            pl.BlockSpec((c_len, hd), lambda b, h, i: (b, vcol + h)),
        ],
        out_specs=pl.BlockSpec((tq, qw), lambda b, h, i: (b * nqt + i, h)),
        out_shape=jax.ShapeDtypeStruct((batch * s_len, nq), BF16),
        compiler_params=_params("arbitrary", "arbitrary", "arbitrary"),
        name="attention_latent",
    )(sinks, qkv_lat, qkv_lat, qkv_lat, qkv_ctx, qkv_ctx)
    ctx = pl.pallas_call(
        functools.partial(_attn_ctx_kernel, c_len=c_len, g_heads=g_heads, hd=hd, scale=scale),
        grid=(batch, n_kv),
        in_specs=[
            smem,
            pl.BlockSpec((c_len, qw), lambda b, h: (b, h)),
            pl.BlockSpec((c_len, hd), lambda b, h: (b, kcol + h)),
            pl.BlockSpec((c_len, hd), lambda b, h: (b, vcol + h)),
        ],
        out_specs=pl.BlockSpec((c_len, qw), lambda b, h: (b, h)),
        out_shape=jax.ShapeDtypeStruct((batch * c_len, nq), BF16),
        compiler_params=_params("arbitrary", "arbitrary"),
        name="attention_context",
    )(sinks, qkv_ctx, qkv_ctx, qkv_ctx)
    return lat, ctx


def _ret_kernel(lg_ref, q_ref, k_ref, v_ref, g_ref, kc_ref, vc_ref, o_ref, s_scr, of_scr, *,
                n_seg, ts, c_len, dv):
    h = pl.program_id(1)
    j = pl.program_id(2)
    ch = RET_CHUNK
    nch = ts // ch
    bwd = j >= n_seg
    seg = jnp.where(bwd, 2 * n_seg - 1 - j, j)
    lg = jnp.where(bwd, lg_ref[1, h], lg_ref[0, h])
    sgn = jnp.where(bwd, -1, 1)

    qi = lax.broadcasted_iota(I32, (ch, ch), 0)
    ki = lax.broadcasted_iota(I32, (ch, ch), 1)
    diff = ((qi - ki) * sgn).astype(F32)
    d_mask = jnp.exp(jnp.where(diff >= 0, diff * lg, -jnp.inf))
    pos = lax.broadcasted_iota(I32, (ch, 1), 0).astype(F32)
    fpos = jnp.where(bwd, ch - 1.0 - pos, pos)
    xi = jnp.exp((fpos + 1.0) * lg)
    zeta = jnp.exp((ch - 1.0 - fpos) * lg)
    chunk_decay = jnp.exp(jnp.full((1, 1), ch, F32) * lg)

    @pl.when((j == 0) | (j == n_seg))
    def _():
        cpos = lax.broadcasted_iota(I32, (c_len, 1), 0).astype(F32)
        w = jnp.exp(jnp.where(bwd, cpos, c_len - 1.0 - cpos) * lg)
        s_scr[...] = lax.dot_general((kc_ref[...] * w).astype(BF16), vc_ref[...].astype(BF16), _TN,
                                     preferred_element_type=F32)

    def chunk(ci, c):
        cc = jnp.where(bwd, nch - 1 - ci, ci)
        r0 = pl.multiple_of(cc * ch, ch)
        rows = pl.ds(r0, ch)
        q = q_ref[rows, :].astype(BF16)
        kf = k_ref[rows, :]
        v = v_ref[rows, :].astype(BF16)
        s = s_scr[...]
        inner = lax.dot_general(q, kf.astype(BF16), _NT, preferred_element_type=F32) * d_mask
        o = jnp.dot(inner.astype(BF16), v, preferred_element_type=F32)
        o = o + jnp.dot(q, s.astype(BF16), preferred_element_type=F32) * xi
        s_scr[...] = chunk_decay * s + lax.dot_general((kf * zeta).astype(BF16), v, _TN,
                                                       preferred_element_type=F32)
        grow = pl.ds(pl.multiple_of(seg * ts + r0, ch), ch)

        @pl.when(jnp.logical_not(bwd))
        def _():
            of_scr[grow, :] = o

        @pl.when(bwd)
        def _():
            t = of_scr[grow, :] + o
            t = t * lax.rsqrt(jnp.sum(t * t, axis=-1, keepdims=True) * (1.0 / dv) + NORM_EPS)
            g = g_ref[rows, :]
            o_ref[rows, :] = (g * jax.nn.sigmoid(g) * t).astype(o_ref.dtype)

        return c

    lax.fori_loop(0, nch, chunk, 0)


def _retention(p_lat, p_ctx, lg, *, batch, s_len, c_len, n_heads, dk, dv):
    qd = n_heads * dk
    vd = n_heads * dv
    ts = _tile(s_len, 1024)
    n_seg = s_len // ts

    def seg_of(j):
        return jnp.where(j >= n_seg, 2 * n_seg - 1 - j, j)

    def seg_bwd(j):
        return jnp.where(j >= n_seg, 2 * n_seg - 1 - j, n_seg - 1)

    kq = qd // dk
    kv_ = (2 * qd) // dv
    kg = (2 * qd + vd) // dv
    return pl.pallas_call(
        functools.partial(_ret_kernel, n_seg=n_seg, ts=ts, c_len=c_len, dv=dv),
        grid=(batch, n_heads, 2 * n_seg),
        in_specs=[
            pl.BlockSpec(memory_space=pltpu.SMEM),
            pl.BlockSpec((ts, dk), lambda b, h, j: (b * n_seg + seg_of(j), h)),
            pl.BlockSpec((ts, dk), lambda b, h, j: (b * n_seg + seg_of(j), kq + h)),
            pl.BlockSpec((ts, dv), lambda b, h, j: (b * n_seg + seg_of(j), kv_ + h)),
            pl.BlockSpec((ts, dv), lambda b, h, j: (b * n_seg + seg_bwd(j), kg + h)),
            pl.BlockSpec((c_len, dk), lambda b, h, j: (b, h)),
            pl.BlockSpec((c_len, dv), lambda b, h, j: (b, qd // dv + h)),
        ],
        out_specs=pl.BlockSpec((ts, dv), lambda b, h, j: (b * n_seg + seg_bwd(j), h)),
        out_shape=jax.ShapeDtypeStruct((batch * s_len, vd), BF16),
        scratch_shapes=[pltpu.VMEM((dk, dv), F32), pltpu.VMEM((s_len, dv), F32)],
        compiler_params=_params("arbitrary", "arbitrary", "arbitrary"),
        name="retention",
    )(lg, p_lat, p_lat, p_lat, p_lat, p_ctx, p_ctx)


def _router_kernel(*refs, n_src, nl_tiles, tm, d, n_exp):
    if n_src == 2:
        hl_ref, hc_ref, wm_ref, sh_ref, w_ref, b_ref, n_ref, g_ref, sr_ref, idx_ref, cnt_ref, carry, nhi, nlo = refs
    else:
        hl_ref, wm_ref, sh_ref, w_ref, b_ref, n_ref, g_ref, sr_ref, idx_ref, cnt_ref, carry, nhi, nlo = refs
        hc_ref = None
    i = pl.program_id(0)

    @pl.when(i == 0)
    def _():
        carry[...] = jnp.zeros_like(carry)

    wm = wm_ref[...]
    sh = sh_ref[...]

    def fill(src_ref):
        def body(r, c):
            rows = pl.ds(pl.multiple_of(r * NORM_ROWS, NORM_ROWS), NORM_ROWS)
            n = _norm_rows(src_ref[rows, :], wm, sh, d)
            n_ref[rows, :] = n
            hi = n.astype(BF16)
            nhi[rows, :] = hi
            nlo[rows, :] = (n - hi.astype(F32)).astype(BF16)
            return c

        lax.fori_loop(0, tm // NORM_ROWS, body, 0)

    if n_src == 2:
        pl.when(i < nl_tiles)(lambda: fill(hl_ref))
        pl.when(i >= nl_tiles)(lambda: fill(hc_ref))
    else:
        fill(hl_ref)

    w_hi = w_ref[0:n_exp, :]
    w_lo = w_ref[n_exp:2 * n_exp, :]
    x_hi = nhi[...]
    logits = (lax.dot_general(w_hi, x_hi, _NT, preferred_element_type=F32)
              + lax.dot_general(w_lo, x_hi, _NT, preferred_element_type=F32)
              + lax.dot_general(w_hi, nlo[...], _NT, preferred_element_type=F32)
              + b_ref[...])

    eidx = lax.broadcasted_iota(I32, (n_exp, tm), 0)
    row8 = lax.broadcasted_iota(I32, (8, tm), 0)
    rest = logits
    sel = jnp.zeros((n_exp, tm), jnp.bool_)
    idx8 = jnp.zeros((8, tm), I32)
    top1 = None
    for k in range(TOP_K):
        mx = jnp.max(rest, axis=0, keepdims=True)
        if k == 0:
            top1 = mx
        am = jnp.min(jnp.where(rest == mx, eidx, n_exp), axis=0, keepdims=True)
        pick = eidx == am
        sel = sel | pick
        idx8 = jnp.where(row8 == k, am, idx8)
        rest = jnp.where(pick, -jnp.inf, rest)
    e = jnp.where(sel, jnp.exp(logits - top1), 0.0)
    g_ref[...] = e / jnp.sum(e, axis=0, keepdims=True)
    idx_ref[...] = idx8

    t0 = lax.broadcasted_iota(I32, (tm, tm), 0)
    t1 = lax.broadcasted_iota(I32, (tm, tm), 1)
    before = jnp.where(t0 < t1, 1.0, 0.0).astype(BF16)
    self_f = jnp.where(sel, 1.0, 0.0)
    rank = jnp.dot(self_f.astype(BF16), before, preferred_element_type=F32) + carry[:, 0:1]
    sr_ref[...] = jnp.where(sel, rank.astype(I32), -1)
    new_carry = carry[...] + jnp.sum(self_f, axis=1, keepdims=True)
    carry[...] = new_carry
    cnt_ref[...] = new_carry


def _router(h_lat, h_ctx, wm, sh, rw, rb, *, group_rows, tm):
    ml, d = h_lat.shape
    n_exp = rw.shape[1]
    mc = 0 if h_ctx is None else h_ctx.shape[0]
    t = ml + mc
    assert ml % tm == 0 and mc % tm == 0 and group_rows % tm == 0
    nl_tiles = ml // tm
    n_groups_lat = ml // group_rows
    rw_t = rw.T
    w_hi = rw_t.astype(BF16)
    w_lo = (rw_t - w_hi.astype(F32)).astype(BF16)
    w3 = jnp.concatenate([w_hi, w_lo], axis=0)

    def grp(i):
        return jnp.where(i < nl_tiles, (i * tm) // group_rows, n_groups_lat)

    in_specs = [pl.BlockSpec((tm, d), lambda i: (jnp.minimum(i, nl_tiles - 1), 0))]
    args = [h_lat]
    if mc:
        in_specs.append(pl.BlockSpec((tm, d), lambda i: (jnp.maximum(i - nl_tiles, 0), 0)))
        args.append(h_ctx)
    in_specs += [
        pl.BlockSpec((None, 1, d), lambda i: (grp(i), 0, 0)),
        pl.BlockSpec((None, 1, d), lambda i: (grp(i), 0, 0)),
        pl.BlockSpec((2 * n_exp, d), lambda i: (0, 0)),
        pl.BlockSpec((n_exp, 1), lambda i: (0, 0)),
    ]
    args += [wm, sh, w3, rb.reshape(n_exp, 1)]
    return pl.pallas_call(
        functools.partial(_router_kernel, n_src=2 if mc else 1, nl_tiles=nl_tiles, tm=tm, d=d, n_exp=n_exp),
        grid=(t // tm,),
        in_specs=in_specs,
        out_specs=[
            pl.BlockSpec((tm, d), lambda i: (i, 0)),
            pl.BlockSpec((n_exp, tm), lambda i: (0, i)),
            pl.BlockSpec((n_exp, tm), lambda i: (0, i)),
            pl.BlockSpec((8, tm), lambda i: (0, i)),
            pl.BlockSpec((n_exp, LANES), lambda i: (0, 0)),
        ],
        out_shape=[
            jax.ShapeDtypeStruct((t, d), F32),
            jax.ShapeDtypeStruct((n_exp, t), F32),
            jax.ShapeDtypeStruct((n_exp, t), I32),
            jax.ShapeDtypeStruct((8, t), I32),
            jax.ShapeDtypeStruct((n_exp, LANES), F32),
        ],
        scratch_shapes=[pltpu.VMEM((n_exp, LANES), F32), pltpu.VMEM((tm, d), BF16), pltpu.VMEM((tm, d), BF16)],
        compiler_params=_params("arbitrary"),
        name="moe_router",
    )(*args)


def _routing_tables(gates, selrank, idx8, cnt, *, tm, n_tiles):
    n_exp, t = selrank.shape
    p = n_tiles * tm
    c = cnt[:, 0].astype(I32)
    pc = ((c + tm - 1) // tm) * tm
    cum = jnp.cumsum(pc)
    off = cum - pc
    pos = jnp.where(selrank >= 0, selrank + off[:, None], p)
    tok = jnp.broadcast_to(jnp.arange(t, dtype=I32)[None, :], (n_exp, t))
    flat = pos.reshape(-1)
    tok_of_slot = jnp.zeros((p,), I32).at[flat].set(tok.reshape(-1), mode="drop")
    gate_of_slot = jnp.zeros((p,), F32).at[flat].set(gates.reshape(-1), mode="drop")
    n_used = (cum[-1] // tm).astype(I32).reshape(1)
    tile_start = jnp.arange(n_tiles, dtype=I32) * tm
    tile_expert = jnp.minimum(jnp.sum(tile_start[:, None] >= cum[None, :], axis=1), n_exp - 1).astype(I32)
    pos4 = jnp.take_along_axis(pos, idx8[:TOP_K], axis=0).T.reshape(-1)
    return tok_of_slot, gate_of_slot.reshape(p, 1), tile_expert, n_used, pos4


def _moe_kernel(te_ref, nu_ref, tok_ref, n_hbm, gw_ref, w1_ref, b1_ref, w2_ref, b2_ref, o_ref, xbuf, sem, *,
                tm, f_dim):
    i = pl.program_id(0)
    n_used = nu_ref[0]
    slot = i % 2

    def issue(tile, sl):
        def body(r, c):
            tok = tok_ref[tile * tm + r]
            pltpu.make_async_copy(n_hbm.at[pl.ds(tok, 1), :], xbuf.at[sl, pl.ds(r, 1), :], sem.at[sl]).start()
            return c

        lax.fori_loop(0, tm, body, 0)

    @pl.when(i == 0)
    def _():
        issue(0, 0)

    @pl.when(i + 1 < n_used)
    def _():
        issue(i + 1, 1 - slot)

    @pl.when(i < n_used)
    def _():
        pltpu.make_async_copy(n_hbm.at[pl.ds(0, tm), :], xbuf.at[slot], sem.at[slot]).wait()
        x = xbuf[slot].astype(BF16)
        gu = jnp.dot(x, w1_ref[...], preferred_element_type=F32) + b1_ref[...]
        gate = jnp.minimum(gu[:, :f_dim], SWIGLU_LIMIT)
        up = jnp.clip(gu[:, f_dim:], -SWIGLU_LIMIT, SWIGLU_LIMIT)
        act = (up + 1.0) * (gate * jax.nn.sigmoid(SWIGLU_ALPHA * gate))
        y = jnp.dot(act.astype(BF16), w2_ref[...], preferred_element_type=F32) + b2_ref[...]
        o_ref[...] = y * gw_ref[...]

    @pl.when(i >= n_used)
    def _():
        o_ref[...] = jnp.zeros_like(o_ref)


def _experts(n_all, tok_of_slot, gate_of_slot, tile_expert, n_used, w1, b1, w2, b2, *, tm, n_tiles):
    n_exp, d, f2 = w1.shape
    f_dim = f2 // 2
    grid_spec = pltpu.PrefetchScalarGridSpec(
        num_scalar_prefetch=3,
        grid=(n_tiles,),
        in_specs=[
            pl.BlockSpec(memory_space=pl.ANY),
            pl.BlockSpec((tm, 1), lambda i, te, nu, tok: (i, 0)),
            pl.BlockSpec((None, d, f2), lambda i, te, nu, tok: (te[i], 0, 0)),
            pl.BlockSpec((None, 1, f2), lambda i, te, nu, tok: (te[i], 0, 0)),
            pl.BlockSpec((None, f_dim, d), lambda i, te, nu, tok: (te[i], 0, 0)),
            pl.BlockSpec((None, 1, d), lambda i, te, nu, tok: (te[i], 0, 0)),
        ],
        out_specs=pl.BlockSpec((tm, d), lambda i, te, nu, tok: (i, 0)),
        scratch_shapes=[pltpu.VMEM((2, tm, d), F32), pltpu.SemaphoreType.DMA((2,))],
    )
    return pl.pallas_call(
        functools.partial(_moe_kernel, tm=tm, f_dim=f_dim),
        grid_spec=grid_spec,
        out_shape=jax.ShapeDtypeStruct((n_tiles * tm, d), F32),
        compiler_params=_params("arbitrary"),
        name="moe_experts",
    )(tile_expert, n_used, tok_of_slot, n_all, gate_of_slot, w1, b1.reshape(n_exp, 1, f2), w2,
      b2.reshape(n_exp, 1, d))


def _combine_kernel(*refs, tc, d, tok_base, final):
    if final:
        pos_ref, ys_hbm, h_ref, g_ref, fw_ref, o_ref, buf, sem = refs
    else:
        pos_ref, ys_hbm, h_ref, g_ref, o_ref, buf, sem = refs
        fw_ref = None
    i = pl.program_id(0)
    nt = pl.num_programs(0)
    slot = i % 2

    def issue(tile, sl):
        def body(r, c):
            base = (tok_base + tile * tc + r) * TOP_K
            for k in range(TOP_K):
                pltpu.make_async_copy(ys_hbm.at[pl.ds(pos_ref[base + k], 1), :],
                                      buf.at[sl, pl.ds(k * tc + r, 1), :], sem.at[sl]).start()
            return c

        lax.fori_loop(0, tc, body, 0)

    @pl.when(i == 0)
    def _():
        issue(0, 0)

    @pl.when(i + 1 < nt)
    def _():
        issue(i + 1, 1 - slot)

    pltpu.make_async_copy(ys_hbm.at[pl.ds(0, TOP_K * tc), :], buf.at[slot], sem.at[slot]).wait()
    gate = g_ref[...]
    fw = None if fw_ref is None else fw_ref[...]

    def body(r, c):
        r0 = pl.multiple_of(r * NORM_ROWS, NORM_ROWS)
        y = buf[slot, pl.ds(r0, NORM_ROWS), :]
        for k in range(1, TOP_K):
            y = y + buf[slot, pl.ds(k * tc + r0, NORM_ROWS), :]
        out = h_ref[pl.ds(r0, NORM_ROWS), :] + gate * y
        if final:
            ms = jnp.sum(out * out, axis=-1, keepdims=True) * (1.0 / d)
            out = out * lax.rsqrt(ms + NORM_EPS) * fw
        o_ref[pl.ds(r0, NORM_ROWS), :] = out
        return c

    lax.fori_loop(0, tc // NORM_ROWS, body, 0)


def _combine(ys, pos4, h, gate, *, group_rows, tok_base, final_w=None):
    m, d = h.shape
    tc = _tile(m, 128)
    assert group_rows % tc == 0
    final = final_w is not None
    in_specs = [
        pl.BlockSpec(memory_space=pl.ANY),
        pl.BlockSpec((tc, d), lambda i, pos: (i, 0)),
        pl.BlockSpec((None, 1, d), lambda i, pos: ((i * tc) // group_rows, 0, 0)),
    ]
    args = [pos4, ys, h, gate]
    if final:
        in_specs.append(pl.BlockSpec((1, d), lambda i, pos: (0, 0)))
        args.append(final_w.reshape(1, d))
    grid_spec = pltpu.PrefetchScalarGridSpec(
        num_scalar_prefetch=1,
        grid=(m // tc,),
        in_specs=in_specs,
        out_specs=pl.BlockSpec((tc, d), lambda i, pos: (i, 0)),
        scratch_shapes=[pltpu.VMEM((2, TOP_K * tc, d), F32), pltpu.SemaphoreType.DMA((2,))],
    )
    return pl.pallas_call(
        functools.partial(_combine_kernel, tc=tc, d=d, tok_base=tok_base, final=final),
        grid_spec=grid_spec,
        out_shape=jax.ShapeDtypeStruct((m, d), F32),
        compiler_params=_params("arbitrary"),
        name="moe_combine",
    )(*args)


def _moe(h_lat, h_ctx, wm, sh, gate, rw, rb, w1, b1, w2, b2, *, s_len, final_w=None):
    ml = h_lat.shape[0]
    mc = 0 if h_ctx is None else h_ctx.shape[0]
    t = ml + mc
    n_exp = rw.shape[1]
    nb = ml // s_len
    tm_r = _tile(math.gcd(ml, mc) if mc else ml, 256)
    n_all, gates, selrank, idx8, cnt = _router(h_lat, h_ctx, wm, sh, rw, rb, group_rows=s_len, tm=tm_r)
    tm = 256
    n_tiles = (TOP_K * t) // tm + n_exp
    tok_of_slot, gate_of_slot, tile_expert, n_used, pos4 = _routing_tables(
        gates, selrank, idx8, cnt, tm=tm, n_tiles=n_tiles)
    ys = _experts(n_all, tok_of_slot, gate_of_slot, tile_expert, n_used, w1, b1, w2, b2, tm=tm, n_tiles=n_tiles)
    new_lat = _combine(ys, pos4, h_lat, gate[:nb], group_rows=s_len, tok_base=0, final_w=final_w)
    new_ctx = None
    if mc:
        new_ctx = _combine(ys, pos4, h_ctx, gate[nb:nb + 1], group_rows=mc, tok_base=ml)
    return new_lat, new_ctx


def _axial_tables(s_len, hd):
    half = hd // 2
    inv = 1.0 / (ROPE_BASE ** (jnp.arange(0, half, 2, dtype=F32) / half))
    t = jnp.arange(s_len)
    rows = (t // GRID_W).astype(F32)[:, None] * inv[None, :]
    cols = (t % GRID_W).astype(F32)[:, None] * inv[None, :]
    cos = jnp.concatenate([jnp.cos(rows), jnp.cos(rows), jnp.cos(cols), jnp.cos(cols)], axis=-1)
    sin = jnp.concatenate([-jnp.sin(rows), jnp.sin(rows), -jnp.sin(cols), jnp.sin(cols)], axis=-1)
    return cos, sin


def _retention_tables(s_len, dk):
    theta = 1.0 / (ROPE_BASE ** jnp.linspace(0.0, 1.0, dk // 2, dtype=F32))
    ang = jnp.arange(s_len, dtype=F32)[:, None] * theta[None, :]
    return jnp.cos(ang), jnp.sin(ang)


def kernel(x, c, ctx, c_ctx, mod_w, mod_b, norm1_w, norm2_w, attn_w_qkv, attn_w_o, attn_sinks, ret_w_in,
           ret_w_out, ret_a_fwd, ret_a_bwd, router_w, router_b, exp_w1, exp_b1, exp_w2, exp_b2, final_norm_w):
    batch, s_len, d = x.shape
    c_len = ctx.shape[1]
    depth = mod_w.shape[0]
    assert depth == 2 and attn_w_qkv.shape[0] == 1 and ret_w_in.shape[0] == 1
    n_heads = attn_sinks.shape[1]
    hd = d // n_heads
    n_kv = (attn_w_qkv.shape[2] // hd - n_heads) // 2
    r_heads = ret_a_fwd.shape[1]
    dk = d // r_heads
    dv = 2 * dk
    qd, vd = r_heads * dk, r_heads * dv
    assert hd == LANES and dk == 2 * LANES
    ml, mc = batch * s_len, batch * c_len

    n_rows = -(-(batch + 1) // 8) * 8
    cvec = jnp.zeros((n_rows, d), F32).at[:batch].set(c).at[batch].set(c_ctx)
    mods = _modulation(cvec, mod_w, mod_b)

    def mod_tables(i):
        sh1, sc1, g1, sh2, sc2, g2 = [a[:batch + 1, None, :] for a in jnp.split(mods[i], 6, axis=-1)]
        wm1 = norm1_w[i][None, None, :] * (1.0 + sc1)
        wm2 = norm2_w[i][None, None, :] * (1.0 + sc2)
        return wm1, sh1, g1, wm2, sh2, g2

    x2 = x.reshape(ml, d)
    ctx2 = ctx.reshape(mc, d)

    wm1, sh1, g1, wm2, sh2, g2 = mod_tables(0)
    w_qkv = attn_w_qkv[0].astype(BF16)
    nq = n_heads * hd
    qkv_lat = _norm_matmul(x2, wm1[:batch], sh1[:batch], w_qkv, group_rows=s_len, out_dtype=BF16,
                           rot="axial", tabs=_axial_tables(s_len, hd), rot_cols=nq + n_kv * hd,
                           name="qkv_latent")
    qkv_ctx = _norm_matmul(ctx2, wm1[batch:], sh1[batch:], w_qkv, group_rows=mc, out_dtype=BF16,
                           name="qkv_context")
    o_lat, o_ctx = _attention(qkv_lat, qkv_ctx, attn_sinks[0], batch=batch, s_len=s_len, c_len=c_len,
                              n_heads=n_heads, n_kv=n_kv, hd=hd)
    w_o = attn_w_o[0].astype(BF16)
    h_lat = _matmul_residual(o_lat, w_o, x2, g1[:batch], group_rows=s_len, name="attn_out_latent")
    h_ctx = _matmul_residual(o_ctx, w_o, ctx2, g1[batch:], group_rows=mc, name="attn_out_context")
    h_lat, h_ctx = _moe(h_lat, h_ctx, wm2, sh2, g2, router_w[0], router_b[0], exp_w1[0].astype(BF16),
                        exp_b1[0], exp_w2[0].astype(BF16), exp_b2[0], s_len=s_len)

    wm1, sh1, g1, wm2, sh2, g2 = mod_tables(1)
    w_in = ret_w_in[0].astype(BF16)
    k_scale = dk ** -0.5
    p_lat = _norm_matmul(h_lat, wm1[:batch], sh1[:batch], w_in, group_rows=s_len, out_dtype=F32,
                         rot="half", tabs=_retention_tables(s_len, dk), rot_cols=2 * qd,
                         k_lo=qd, k_hi=2 * qd, k_scale=k_scale, name="ret_in_latent")
    p_ctx = _norm_matmul(h_ctx, wm1[batch:], sh1[batch:], w_in, group_rows=mc, out_dtype=F32,
                         col_off=qd, n_out=qd + vd, k_lo=qd, k_hi=2 * qd, k_scale=k_scale,
                         name="ret_in_context")
    lg = jnp.stack([jnp.log1p(-jnp.exp(ret_a_fwd[0].astype(F32))), jnp.log1p(-jnp.exp(ret_a_bwd[0].astype(F32)))])
    r_lat = _retention(p_lat, p_ctx, lg, batch=batch, s_len=s_len, c_len=c_len, n_heads=r_heads, dk=dk, dv=dv)
    h_lat = _matmul_residual(r_lat, ret_w_out[0].astype(BF16), h_lat, g1[:batch], group_rows=s_len,
                             name="ret_out_latent")
    out, _ = _moe(h_lat, None, wm2, sh2, g2, router_w[1], router_b[1], exp_w1[1].astype(BF16), exp_b1[1],
                  exp_w2[1].astype(BF16), exp_b2[1], s_len=s_len, final_w=final_norm_w)
    return out.reshape(batch, s_len, d)
```

```python
import functools
import math

import jax
import jax.numpy as jnp
from jax import lax
from jax.experimental import pallas as pl
from jax.experimental.pallas import tpu as pltpu

F32 = jnp.float32
BF16 = jnp.bfloat16
I32 = jnp.int32

GRID_W = 64
ATTN_WINDOW = 128
ATTN_BLOCK = 128
RET_CHUNK = 128
ROPE_BASE = 10000.0
TOP_K = 4
SWIGLU_LIMIT = 7.0
SWIGLU_ALPHA = 1.702
NORM_EPS = 1e-6

LANES = 128
NORM_ROWS = 16
DMA_UNROLL = 32
MM_WEIGHT_ELEMS = 4 * 1024 * 1024
MM_OUT_BLOCK_BYTES = 2 * 1024 * 1024
VMEM_LIMIT = 58 * 1024 * 1024

_NT = (((1,), (1,)), ((), ()))
_TN = (((0,), (0,)), ((), ()))


def _params(*sem):
    return pltpu.CompilerParams(dimension_semantics=sem, vmem_limit_bytes=VMEM_LIMIT)


def _tile(dim, pref, *also):
    t = min(dim, pref)
    while any(e % t for e in (dim,) + also):
        t //= 2
    assert t >= 8, (dim, pref, also)
    return t


def _mod_kernel(s_ref, w_ref, b_ref, o_ref):
    s = s_ref[...]
    s = s * jax.nn.sigmoid(s)
    o_ref[...] = jnp.dot(s.astype(BF16), w_ref[...].astype(BF16), preferred_element_type=F32) + b_ref[...]


def _modulation(cvec, mod_w, mod_b):
    nl, d, n = mod_w.shape
    r = cvec.shape[0]
    tn = _tile(n, 512)
    return pl.pallas_call(
        _mod_kernel,
        grid=(nl, n // tn),
        in_specs=[
            pl.BlockSpec((r, d), lambda l, j: (0, 0)),
            pl.BlockSpec((None, d, tn), lambda l, j: (l, 0, j)),
            pl.BlockSpec((None, 1, tn), lambda l, j: (l, 0, j)),
        ],
        out_specs=pl.BlockSpec((None, r, tn), lambda l, j: (l, 0, j)),
        out_shape=jax.ShapeDtypeStruct((nl, r, n), F32),
        compiler_params=_params("arbitrary", "arbitrary"),
        name="modulation",
    )(cvec, mod_w, mod_b.reshape(nl, 1, n))


def _norm_rows(x, wm, sh, d):
    ms = jnp.sum(x * x, axis=-1, keepdims=True) * (1.0 / d)
    return x * lax.rsqrt(ms + NORM_EPS) * wm + sh


def _nmm_kernel(*refs, d, tm, tn, col_off, rot, rot_cols, k_lo, k_hi, k_scale):
    if rot is None:
        x_ref, wm_ref, sh_ref, w_ref, o_ref, n_scr = refs
        cos_ref = sin_ref = None
    else:
        x_ref, wm_ref, sh_ref, w_ref, cos_ref, sin_ref, o_ref, n_scr = refs
    j = pl.program_id(1)

    @pl.when(j == 0)
    def _():
        wm = wm_ref[...]
        sh = sh_ref[...]

        def body(r, c):
            rows = pl.ds(pl.multiple_of(r * NORM_ROWS, NORM_ROWS), NORM_ROWS)
            n_scr[rows, :] = _norm_rows(x_ref[rows, :], wm, sh, d).astype(BF16)
            return c

        lax.fori_loop(0, tm // NORM_ROWS, body, 0)

    acc = jnp.dot(n_scr[...], w_ref[...], preferred_element_type=F32)
    col0 = j * tn + col_off
    if k_scale is not None:
        acc = acc * jnp.where((col0 >= k_lo) & (col0 < k_hi), k_scale, 1.0).astype(F32)

    if rot is None:
        o_ref[...] = acc.astype(o_ref.dtype)
        return

    @pl.when(col0 >= rot_cols)
    def _():
        o_ref[...] = acc.astype(o_ref.dtype)

    @pl.when(col0 < rot_cols)
    def _():
        cos = cos_ref[...]
        sin = sin_ref[...]
        if rot == "axial":
            lane = lax.broadcasted_iota(I32, (tm, LANES), 1)
            first = (lane & 32) == 0
            for hh in range(tn // LANES):
                a = acc[:, hh * LANES:(hh + 1) * LANES]
                partner = jnp.where(first, pltpu.roll(a, LANES - 32, 1), pltpu.roll(a, 32, 1))
                o_ref[:, hh * LANES:(hh + 1) * LANES] = (a * cos + partner * sin).astype(o_ref.dtype)
        else:
            for hh in range(tn // (2 * LANES)):
                c0 = hh * 2 * LANES
                x1 = acc[:, c0:c0 + LANES]
                x2 = acc[:, c0 + LANES:c0 + 2 * LANES]
                o_ref[:, c0:c0 + LANES] = (x1 * cos - x2 * sin).astype(o_ref.dtype)
                o_ref[:, c0 + LANES:c0 + 2 * LANES] = (x1 * sin + x2 * cos).astype(o_ref.dtype)


def _norm_matmul(h, wm, sh, w, *, group_rows, out_dtype, col_off=0, n_out=None, rot=None, tabs=None,
                 rot_cols=0, k_lo=0, k_hi=0, k_scale=None, name="norm_matmul"):
    m, d = h.shape
    n_out = w.shape[1] - col_off if n_out is None else n_out
    tm = _tile(m, 1024, group_rows)
    tn = min(MM_WEIGHT_ELEMS // d, MM_OUT_BLOCK_BYTES // (tm * jnp.dtype(out_dtype).itemsize))
    tn = _tile(n_out, tn, col_off, rot_cols, k_lo, k_hi)
    in_specs = [
        pl.BlockSpec((tm, d), lambda i, j: (i, 0), pipeline_mode=pl.Buffered(1)),
        pl.BlockSpec((None, 1, d), lambda i, j: ((i * tm) // group_rows, 0, 0)),
        pl.BlockSpec((None, 1, d), lambda i, j: ((i * tm) // group_rows, 0, 0)),
        pl.BlockSpec((d, tn), lambda i, j: (0, j + col_off // tn)),
    ]
    args = [h, wm, sh, w]
    if rot is not None:
        cos, sin = tabs
        s_pos = cos.shape[0]
        assert s_pos % tm == 0 and rot_cols % tn == 0
        nper = s_pos // tm
        in_specs += [pl.BlockSpec((tm, LANES), lambda i, j: (i % nper, 0))] * 2
        args += [cos, sin]
    kern = functools.partial(_nmm_kernel, d=d, tm=tm, tn=tn, col_off=col_off, rot=rot, rot_cols=rot_cols,
                             k_lo=k_lo, k_hi=k_hi, k_scale=k_scale)
    return pl.pallas_call(
        kern,
        grid=(m // tm, n_out // tn),
        in_specs=in_specs,
        out_specs=pl.BlockSpec((tm, tn), lambda i, j: (i, j)),
        out_shape=jax.ShapeDtypeStruct((m, n_out), out_dtype),
        scratch_shapes=[pltpu.VMEM((tm, d), BF16)],
        compiler_params=_params("arbitrary", "arbitrary"),
        name=name,
    )(*args)


def _mmres_kernel(a_ref, w_ref, h_ref, g_ref, o_ref):
    acc = jnp.dot(a_ref[...], w_ref[...], preferred_element_type=F32)
    o_ref[...] = h_ref[...] + g_ref[...] * acc


def _matmul_residual(a, w, h, gate, *, group_rows, name="matmul_residual"):
    m, k = a.shape
    n = w.shape[1]
    tm = _tile(m, 1024, group_rows)
    tn = _tile(n, max(LANES, MM_WEIGHT_ELEMS // k))
    return pl.pallas_call(
        _mmres_kernel,
        grid=(m // tm, n // tn),
        in_specs=[
            pl.BlockSpec((tm, k), lambda i, j: (i, 0), pipeline_mode=pl.Buffered(1)),
            pl.BlockSpec((k, tn), lambda i, j: (0, j)),
            pl.BlockSpec((tm, tn), lambda i, j: (i, j)),
            pl.BlockSpec((None, 1, tn), lambda i, j: ((i * tm) // group_rows, 0, j)),
        ],
        out_specs=pl.BlockSpec((tm, tn), lambda i, j: (i, j)),
        out_shape=jax.ShapeDtypeStruct((m, n), F32),
        compiler_params=_params("arbitrary", "arbitrary"),
        name=name,
    )(a, w, h, gate)


def _softmax_pv(s_parts, v_parts, sink_col):
    m = sink_col
    for s in s_parts:
        m = jnp.maximum(m, jnp.max(s, axis=-1, keepdims=True))
    es = [jnp.exp(s - m) for s in s_parts]
    den = jnp.exp(sink_col - m)
    for e in es:
        den = den + jnp.sum(e, axis=-1, keepdims=True)
    inv = 1.0 / den
    o = None
    for e, v in zip(es, v_parts):
        t = jnp.dot((e * inv).astype(BF16), v, preferred_element_type=F32)
        o = t if o is None else o + t
    return o


def _sink_column(sink_ref, kv, g_heads, rows):
    return jnp.concatenate([jnp.full((rows, 1), sink_ref[kv * g_heads + g], F32) for g in range(g_heads)], axis=0)


def _attn_lat_kernel(sink_ref, q_ref, k_ref, v_ref, kc_ref, vc_ref, o_ref, *, s_len, tq, g_heads, hd, scale):
    kv = pl.program_id(1)
    qi = pl.program_id(2)
    blk = ATTN_BLOCK
    win = 3 * blk
    nblk = tq // blk
    kc = kc_ref[...]
    vc = vc_ref[...]
    sink_col = _sink_column(sink_ref, kv, g_heads, blk)
    rows = lax.broadcasted_iota(I32, (g_heads * blk, win), 0) & (blk - 1)
    cols = lax.broadcasted_iota(I32, (g_heads * blk, win), 1)
    rel = rows - cols + ATTN_WINDOW

    def body(bi, c):
        n = qi * nblk + bi
        start = pl.multiple_of(jnp.clip((n - 1) * blk, 0, s_len - win), blk)
        delta = n * blk - start
        r0 = pl.multiple_of(bi * blk, blk)
        q4 = jnp.concatenate([q_ref[pl.ds(r0, blk), g * hd:(g + 1) * hd] for g in range(g_heads)], axis=0)
        kw = k_ref[pl.ds(start, win), :]
        vw = v_ref[pl.ds(start, win), :]
        s_loc = lax.dot_general(q4, kw, _NT, preferred_element_type=F32) * scale
        dist = rel + delta
        s_loc = jnp.where((dist >= 0) & (dist <= 2 * ATTN_WINDOW), s_loc, -jnp.inf)
        s_ctx = lax.dot_general(q4, kc, _NT, preferred_element_type=F32) * scale
        o = _softmax_pv([s_loc, s_ctx], [vw, vc], sink_col)
        for g in range(g_heads):
            o_ref[pl.ds(r0, blk), g * hd:(g + 1) * hd] = o[g * blk:(g + 1) * blk].astype(o_ref.dtype)
        return c

    lax.fori_loop(0, nblk, body, 0)


def _attn_ctx_kernel(sink_ref, q_ref, kc_ref, vc_ref, o_ref, *, c_len, g_heads, hd, scale):
    kv = pl.program_id(1)
    sink_col = _sink_column(sink_ref, kv, g_heads, c_len)
    q4 = jnp.concatenate([q_ref[:, g * hd:(g + 1) * hd] for g in range(g_heads)], axis=0)
    kc = kc_ref[...]
    s = lax.dot_general(q4, kc, _NT, preferred_element_type=F32) * scale
    o = _softmax_pv([s], [vc_ref[...]], sink_col)
    for g in range(g_heads):
        o_ref[:, g * hd:(g + 1) * hd] = o[g * c_len:(g + 1) * c_len].astype(o_ref.dtype)


def _attention(qkv_lat, qkv_ctx, sinks, *, batch, s_len, c_len, n_heads, n_kv, hd):
    g_heads = n_heads // n_kv
    nq = n_heads * hd
    scale = hd ** -0.5
    tq = _tile(s_len, 1024)
    nqt = s_len // tq
    qw = g_heads * hd
    kcol = nq // hd
    vcol = kcol + n_kv
    smem = pl.BlockSpec(memory_space=pltpu.SMEM)
    lat = pl.pallas_call(
        functools.partial(_attn_lat_kernel, s_len=s_len, tq=tq, g_heads=g_heads, hd=hd, scale=scale),
        grid=(batch, n_kv, nqt),
        in_specs=[
            smem,
            pl.BlockSpec((tq, qw), lambda b, h, i: (b * nqt + i, h)),
            pl.BlockSpec((s_len, hd), lambda b, h, i: (b, kcol + h)),
            pl.BlockSpec((s_len, hd), lambda b, h, i: (b, vcol + h)),
            pl.BlockSpec((c_len, hd), lambda b, h, i: (b, kcol + h)),
            pl.BlockSpec((c_len, hd), lambda b, h, i: (b, vcol + h)),
        ],
        out_specs=pl.BlockSpec((tq, qw), lambda b, h, i: (b * nqt + i, h)),
        out_shape=jax.ShapeDtypeStruct((batch * s_len, nq), BF16),
        compiler_params=_params("arbitrary", "arbitrary", "arbitrary"),
        name="attention_latent",
    )(sinks, qkv_lat, qkv_lat, qkv_lat, qkv_ctx, qkv_ctx)
    ctx = pl.pallas_call(
        functools.partial(_attn_ctx_kernel, c_len=c_len, g_heads=g_heads, hd=hd, scale=scale),
        grid=(batch, n_kv),
        in_specs=[
            smem,
            pl.BlockSpec((c_len, qw), lambda b, h: (b, h)),
            pl.BlockSpec((c_len, hd), lambda b, h: (b, kcol + h)),
            pl.BlockSpec((c_len, hd), lambda b, h: (b, vcol + h)),
        ],
        out_specs=pl.BlockSpec((c_len, qw), lambda b, h: (b, h)),
        out_shape=jax.ShapeDtypeStruct((batch * c_len, nq), BF16),
        compiler_params=_params("arbitrary", "arbitrary"),
        name="attention_context",
    )(sinks, qkv_ctx, qkv_ctx, qkv_ctx)
    return lat, ctx


def _ret_kernel(lg_ref, q_ref, k_ref, v_ref, g_ref, kc_ref, vc_ref, o_ref, s_scr, of_scr, *,
                n_seg, ts, c_len, dv):
    h = pl.program_id(1)
    j = pl.program_id(2)
    ch = RET_CHUNK
    nch = ts // ch
    bwd = j >= n_seg
    seg = jnp.where(bwd, 2 * n_seg - 1 - j, j)
    lg = jnp.where(bwd, lg_ref[1, h], lg_ref[0, h])
    sgn = jnp.where(bwd, -1, 1)

    qi = lax.broadcasted_iota(I32, (ch, ch), 0)
    ki = lax.broadcasted_iota(I32, (ch, ch), 1)
    diff = ((qi - ki) * sgn).astype(F32)
    d_mask = jnp.exp(jnp.where(diff >= 0, diff * lg, -jnp.inf))
    pos = lax.broadcasted_iota(I32, (ch, 1), 0).astype(F32)
    fpos = jnp.where(bwd, ch - 1.0 - pos, pos)
    xi = jnp.exp((fpos + 1.0) * lg)
    zeta = jnp.exp((ch - 1.0 - fpos) * lg)
    chunk_decay = jnp.exp(jnp.full((1, 1), ch, F32) * lg)

    @pl.when((j == 0) | (j == n_seg))
    def _():
        cpos = lax.broadcasted_iota(I32, (c_len, 1), 0).astype(F32)
        w = jnp.exp(jnp.where(bwd, cpos, c_len - 1.0 - cpos) * lg)
        s_scr[...] = lax.dot_general((kc_ref[...] * w).astype(BF16), vc_ref[...].astype(BF16), _TN,
                                     preferred_element_type=F32)

    def chunk(ci, c):
        cc = jnp.where(bwd, nch - 1 - ci, ci)
        r0 = pl.multiple_of(cc * ch, ch)
        rows = pl.ds(r0, ch)
        q = q_ref[rows, :].astype(BF16)
        kf = k_ref[rows, :]
        v = v_ref[rows, :].astype(BF16)
        s = s_scr[...]
        inner = lax.dot_general(q, kf.astype(BF16), _NT, preferred_element_type=F32) * d_mask
        o = jnp.dot(inner.astype(BF16), v, preferred_element_type=F32)
        o = o + jnp.dot(q, s.astype(BF16), preferred_element_type=F32) * xi
        s_scr[...] = chunk_decay * s + lax.dot_general((kf * zeta).astype(BF16), v, _TN,
                                                       preferred_element_type=F32)
        grow = pl.ds(pl.multiple_of(seg * ts + r0, ch), ch)

        @pl.when(jnp.logical_not(bwd))
        def _():
            of_scr[grow, :] = o

        @pl.when(bwd)
        def _():
            t = of_scr[grow, :] + o
            t = t * lax.rsqrt(jnp.sum(t * t, axis=-1, keepdims=True) * (1.0 / dv) + NORM_EPS)
            g = g_ref[rows, :]
            o_ref[rows, :] = (g * jax.nn.sigmoid(g) * t).astype(o_ref.dtype)

        return c

    lax.fori_loop(0, nch, chunk, 0)


def _retention(p_lat, p_ctx, lg, *, batch, s_len, c_len, n_heads, dk, dv):
    qd = n_heads * dk
    vd = n_heads * dv
    ts = _tile(s_len, 1024)
    n_seg = s_len // ts

    def seg_of(j):
        return jnp.where(j >= n_seg, 2 * n_seg - 1 - j, j)

    def seg_bwd(j):
        return jnp.where(j >= n_seg, 2 * n_seg - 1 - j, n_seg - 1)

    kq = qd // dk
    kv_ = (2 * qd) // dv
    kg = (2 * qd + vd) // dv
    return pl.pallas_call(
        functools.partial(_ret_kernel, n_seg=n_seg, ts=ts, c_len=c_len, dv=dv),
        grid=(batch, n_heads, 2 * n_seg),
        in_specs=[
            pl.BlockSpec(memory_space=pltpu.SMEM),
            pl.BlockSpec((ts, dk), lambda b, h, j: (b * n_seg + seg_of(j), h)),
            pl.BlockSpec((ts, dk), lambda b, h, j: (b * n_seg + seg_of(j), kq + h)),
            pl.BlockSpec((ts, dv), lambda b, h, j: (b * n_seg + seg_of(j), kv_ + h)),
            pl.BlockSpec((ts, dv), lambda b, h, j: (b * n_seg + seg_bwd(j), kg + h)),
            pl.BlockSpec((c_len, dk), lambda b, h, j: (b, h)),
            pl.BlockSpec((c_len, dv), lambda b, h, j: (b, qd // dv + h)),
        ],
        out_specs=pl.BlockSpec((ts, dv), lambda b, h, j: (b * n_seg + seg_bwd(j), h)),
        out_shape=jax.ShapeDtypeStruct((batch * s_len, vd), BF16),
        scratch_shapes=[pltpu.VMEM((dk, dv), F32), pltpu.VMEM((s_len, dv), F32)],
        compiler_params=_params("arbitrary", "arbitrary", "arbitrary"),
        name="retention",
    )(lg, p_lat, p_lat, p_lat, p_lat, p_ctx, p_ctx)


def _router_kernel(*refs, n_src, nl_tiles, tm, d, n_exp):
    if n_src == 2:
        hl_ref, hc_ref, wm_ref, sh_ref, w_ref, b_ref, n_ref, idx_ref, w8_ref, r8_ref, cnt_ref, carry, nhi, nlo = refs
    else:
        hl_ref, wm_ref, sh_ref, w_ref, b_ref, n_ref, idx_ref, w8_ref, r8_ref, cnt_ref, carry, nhi, nlo = refs
        hc_ref = None
    i = pl.program_id(0)
    dh = d // 2

    @pl.when(i == 0)
    def _():
        carry[...] = jnp.zeros_like(carry)

    wm = wm_ref[...]
    sh = sh_ref[...]

    def fill(src_ref):
        def body(r, c):
            rows = pl.ds(pl.multiple_of(r * NORM_ROWS, NORM_ROWS), NORM_ROWS)
            n = _norm_rows(src_ref[rows, :], wm, sh, d)
            hi = n.astype(BF16)
            hi32 = hi.astype(F32)
            bits = lax.bitcast_convert_type(hi32, jnp.uint32)
            n_ref[rows, :] = (bits[:, :dh] >> 16) | (bits[:, dh:] & jnp.uint32(0xFFFF0000))
            nhi[rows, :] = hi
            nlo[rows, :] = (n - hi32).astype(BF16)
            return c

        lax.fori_loop(0, tm // NORM_ROWS, body, 0)

    if n_src == 2:
        pl.when(i < nl_tiles)(lambda: fill(hl_ref))
        pl.when(i >= nl_tiles)(lambda: fill(hc_ref))
    else:
        fill(hl_ref)

    w_hi = w_ref[0:n_exp, :]
    w_lo = w_ref[n_exp:2 * n_exp, :]
    x_hi = nhi[...]
    logits = (lax.dot_general(w_hi, x_hi, _NT, preferred_element_type=F32)
              + lax.dot_general(w_lo, x_hi, _NT, preferred_element_type=F32)
              + lax.dot_general(w_hi, nlo[...], _NT, preferred_element_type=F32)
              + b_ref[...])

    eidx = lax.broadcasted_iota(I32, (n_exp, tm), 0)
    row8 = lax.broadcasted_iota(I32, (8, tm), 0)
    rest = logits
    sel = jnp.zeros((n_exp, tm), jnp.bool_)
    picks = []
    idx8 = jnp.zeros((8, tm), I32)
    top1 = None
    for k in range(TOP_K):
        mx = jnp.max(rest, axis=0, keepdims=True)
        if k == 0:
            top1 = mx
        am = jnp.min(jnp.where(rest == mx, eidx, n_exp), axis=0, keepdims=True)
        pick = eidx == am
        picks.append(pick)
        sel = sel | pick
        idx8 = jnp.where(row8 == k, am, idx8)
        rest = jnp.where(pick, -jnp.inf, rest)
    e = jnp.where(sel, jnp.exp(logits - top1), 0.0)
    gates = e / jnp.sum(e, axis=0, keepdims=True)

    t0 = lax.broadcasted_iota(I32, (tm, tm), 0)
    t1 = lax.broadcasted_iota(I32, (tm, tm), 1)
    before = jnp.where(t0 < t1, 1.0, 0.0).astype(BF16)
    self_f = jnp.where(sel, 1.0, 0.0)
    rank = jnp.dot(self_f.astype(BF16), before, preferred_element_type=F32) + carry[:, 0:1]
    w8 = jnp.zeros((8, tm), F32)
    r8 = jnp.zeros((8, tm), F32)
    for k, pick in enumerate(picks):
        w8 = jnp.where(row8 == k, jnp.sum(jnp.where(pick, gates, 0.0), axis=0, keepdims=True), w8)
        r8 = jnp.where(row8 == k, jnp.sum(jnp.where(pick, rank, 0.0), axis=0, keepdims=True), r8)
    idx_ref[...] = idx8
    w8_ref[...] = w8
    r8_ref[...] = r8.astype(I32)
    new_carry = carry[...] + jnp.sum(self_f, axis=1, keepdims=True)
    carry[...] = new_carry
    cnt_ref[...] = new_carry


def _router(h_lat, h_ctx, wm, sh, rw, rb, *, group_rows, tm):
    ml, d = h_lat.shape
    n_exp = rw.shape[1]
    mc = 0 if h_ctx is None else h_ctx.shape[0]
    t = ml + mc
    assert ml % tm == 0 and mc % tm == 0 and group_rows % tm == 0
    nl_tiles = ml // tm
    n_groups_lat = ml // group_rows
    rw_t = rw.T
    w_hi = rw_t.astype(BF16)
    w_lo = (rw_t - w_hi.astype(F32)).astype(BF16)
    w3 = jnp.concatenate([w_hi, w_lo], axis=0)

    def grp(i):
        return jnp.where(i < nl_tiles, (i * tm) // group_rows, n_groups_lat)

    in_specs = [pl.BlockSpec((tm, d), lambda i: (jnp.minimum(i, nl_tiles - 1), 0))]
    args = [h_lat]
    if mc:
        in_specs.append(pl.BlockSpec((tm, d), lambda i: (jnp.maximum(i - nl_tiles, 0), 0)))
        args.append(h_ctx)
    in_specs += [
        pl.BlockSpec((None, 1, d), lambda i: (grp(i), 0, 0)),
        pl.BlockSpec((None, 1, d), lambda i: (grp(i), 0, 0)),
        pl.BlockSpec((2 * n_exp, d), lambda i: (0, 0)),
        pl.BlockSpec((n_exp, 1), lambda i: (0, 0)),
    ]
    args += [wm, sh, w3, rb.reshape(n_exp, 1)]
    return pl.pallas_call(
        functools.partial(_router_kernel, n_src=2 if mc else 1, nl_tiles=nl_tiles, tm=tm, d=d, n_exp=n_exp),
        grid=(t // tm,),
        in_specs=in_specs,
        out_specs=[
            pl.BlockSpec((tm, d // 2), lambda i: (i, 0)),
            pl.BlockSpec((8, tm), lambda i: (0, i)),
            pl.BlockSpec((8, tm), lambda i: (0, i)),
            pl.BlockSpec((8, tm), lambda i: (0, i)),
            pl.BlockSpec((n_exp, LANES), lambda i: (0, 0)),
        ],
        out_shape=[
            jax.ShapeDtypeStruct((t, d // 2), jnp.uint32),
            jax.ShapeDtypeStruct((8, t), I32),
            jax.ShapeDtypeStruct((8, t), F32),
            jax.ShapeDtypeStruct((8, t), I32),
            jax.ShapeDtypeStruct((n_exp, LANES), F32),
        ],
        scratch_shapes=[pltpu.VMEM((n_exp, LANES), F32), pltpu.VMEM((tm, d), BF16), pltpu.VMEM((tm, d), BF16)],
        compiler_params=_params("arbitrary"),
        name="moe_router",
    )(*args)


def _routing_tables(idx8, r8, cnt, *, tm, n_tiles):
    n_exp = cnt.shape[0]
    c = cnt[:, 0].astype(I32)
    pc = ((c + tm - 1) // tm) * tm
    cum = jnp.cumsum(pc)
    off = cum - pc
    idx4 = idx8[:TOP_K]
    off4 = jnp.sum(jnp.where(idx4[:, :, None] == jnp.arange(n_exp, dtype=I32), off, 0), axis=-1)
    pos4 = (off4 + r8[:TOP_K]).T.reshape(-1)
    extents = jnp.stack([cum[-1] // tm, jnp.asarray(n_tiles * tm, I32), jnp.asarray(idx8.shape[1], I32)]).astype(I32)
    tile_start = jnp.arange(n_tiles, dtype=I32) * tm
    tile_expert = jnp.minimum(jnp.sum(tile_start[:, None] >= cum[None, :], axis=1), n_exp - 1).astype(I32)
    return tile_expert, extents, pos4


def _moe_kernel(te_ref, nu_ref, pos_ref, n_hbm, w1_ref, b1_ref, w2_ref, b2_ref, o_ref, tok_ref, xbuf, sem, *,
                tm, f_dim, dh):
    i = pl.program_id(0)
    n_used = nu_ref[0]
    slot = i % 2

    def row_copy(tile, r, sl):
        tok = tok_ref[tile * tm + r]
        return pltpu.make_async_copy(n_hbm.at[pl.ds(tok, 1), :], xbuf.at[sl, pl.ds(r, 1), :], sem.at[sl])

    def wait_rows(sl):
        pltpu.make_async_copy(n_hbm.at[pl.ds(0, tm), :], xbuf.at[sl], sem.at[sl]).wait()

    def start_rows(tile, sl):
        def body(rb, c):
            for u in range(DMA_UNROLL):
                row_copy(tile, rb * DMA_UNROLL + u, sl).start()
            return c

        lax.fori_loop(0, tm // DMA_UNROLL, body, 0)

    @pl.when(i == 0)
    def _():
        def zero(jb, c):
            for u in range(DMA_UNROLL):
                tok_ref[jb * DMA_UNROLL + u] = 0
            return c

        lax.fori_loop(0, nu_ref[1] // DMA_UNROLL, zero, 0)

        def invert(t, c):
            for k in range(TOP_K):
                tok_ref[pos_ref[t * TOP_K + k]] = t
            return c

        lax.fori_loop(0, nu_ref[2], invert, 0)
        start_rows(0, 0)

    @pl.when(i < n_used)
    def _():
        start_rows(jnp.minimum(i + 1, n_used - 1), 1 - slot)
        wait_rows(slot)
        words = xbuf[slot]
        x_lo = lax.bitcast_convert_type(words << 16, F32).astype(BF16)
        x_hi = lax.bitcast_convert_type(words & jnp.uint32(0xFFFF0000), F32).astype(BF16)
        gu = (jnp.dot(x_lo, w1_ref[0:dh, :], preferred_element_type=F32)
              + jnp.dot(x_hi, w1_ref[dh:2 * dh, :], preferred_element_type=F32) + b1_ref[...])
        gate = jnp.minimum(gu[:, :f_dim], SWIGLU_LIMIT)
        up = jnp.clip(gu[:, f_dim:], -SWIGLU_LIMIT, SWIGLU_LIMIT)
        act = (up + 1.0) * (gate * jax.nn.sigmoid(SWIGLU_ALPHA * gate))
        o_ref[...] = jnp.dot(act.astype(BF16), w2_ref[...], preferred_element_type=F32) + b2_ref[...]

        @pl.when(i == n_used - 1)
        def _():
            wait_rows(1 - slot)

    @pl.when(i >= n_used)
    def _():
        o_ref[...] = jnp.zeros_like(o_ref)


def _experts(n_packed, tile_expert, n_used, pos4, w1, b1, w2, b2, *, tm, n_tiles):
    n_exp, d, f2 = w1.shape
    f_dim = f2 // 2
    dh = n_packed.shape[1]
    grid_spec = pltpu.PrefetchScalarGridSpec(
        num_scalar_prefetch=3,
        grid=(n_tiles,),
        in_specs=[
            pl.BlockSpec(memory_space=pl.ANY),
            pl.BlockSpec((None, d, f2), lambda i, te, nu, pos: (te[i], 0, 0)),
            pl.BlockSpec((None, 1, f2), lambda i, te, nu, pos: (te[i], 0, 0)),
            pl.BlockSpec((None, f_dim, d), lambda i, te, nu, pos: (te[i], 0, 0)),
            pl.BlockSpec((None, 1, d), lambda i, te, nu, pos: (te[i], 0, 0)),
        ],
        out_specs=pl.BlockSpec((tm, d), lambda i, te, nu, pos: (i, 0)),
        scratch_shapes=[pltpu.SMEM((n_tiles * tm,), I32), pltpu.VMEM((2, tm, dh), jnp.uint32),
                        pltpu.SemaphoreType.DMA((2,))],
    )
    return pl.pallas_call(
        functools.partial(_moe_kernel, tm=tm, f_dim=f_dim, dh=dh),
        grid_spec=grid_spec,
        out_shape=jax.ShapeDtypeStruct((n_tiles * tm, d), F32),
        compiler_params=_params("arbitrary"),
        name="moe_experts",
    )(tile_expert, n_used, pos4, n_packed, w1, b1.reshape(n_exp, 1, f2), w2, b2.reshape(n_exp, 1, d))


def _combine_kernel(*refs, tc, d, tok_base, final):
    if final:
        pos_ref, ys_hbm, h_ref, g_ref, w_ref, fw_ref, o_ref, buf, sem = refs
    else:
        pos_ref, ys_hbm, h_ref, g_ref, w_ref, o_ref, buf, sem = refs
        fw_ref = None
    i = pl.program_id(0)
    nt = pl.num_programs(0)
    slot = i % 2

    def issue(tile, sl):
        def body(rb, c):
            for u in range(DMA_UNROLL // TOP_K):
                r = rb * (DMA_UNROLL // TOP_K) + u
                base = (tok_base + tile * tc + r) * TOP_K
                for k in range(TOP_K):
                    pltpu.make_async_copy(ys_hbm.at[pl.ds(pos_ref[base + k], 1), :],
                                          buf.at[sl, pl.ds(k * tc + r, 1), :], sem.at[sl]).start()
            return c

        lax.fori_loop(0, tc // (DMA_UNROLL // TOP_K), body, 0)

    @pl.when(i == 0)
    def _():
        issue(0, 0)

    @pl.when(i + 1 < nt)
    def _():
        issue(i + 1, 1 - slot)

    pltpu.make_async_copy(ys_hbm.at[pl.ds(0, TOP_K * tc), :], buf.at[slot], sem.at[slot]).wait()
    gate = g_ref[...]
    fw = None if fw_ref is None else fw_ref[...]

    def body(r, c):
        r0 = pl.multiple_of(r * NORM_ROWS, NORM_ROWS)
        wk = w_ref[pl.ds(r0, NORM_ROWS), :]
        y = wk[:, 0:1] * buf[slot, pl.ds(r0, NORM_ROWS), :]
        for k in range(1, TOP_K):
            y = y + wk[:, k:k + 1] * buf[slot, pl.ds(k * tc + r0, NORM_ROWS), :]
        out = h_ref[pl.ds(r0, NORM_ROWS), :] + gate * y
        if final:
            ms = jnp.sum(out * out, axis=-1, keepdims=True) * (1.0 / d)
            out = out * lax.rsqrt(ms + NORM_EPS) * fw
        o_ref[pl.ds(r0, NORM_ROWS), :] = out
        return c

    lax.fori_loop(0, tc // NORM_ROWS, body, 0)


def _combine(ys, pos4, w4, h, gate, *, group_rows, tok_base, final_w=None):
    m, d = h.shape
    tc = _tile(m, 128)
    assert group_rows % tc == 0 and tok_base % tc == 0
    final = final_w is not None
    in_specs = [
        pl.BlockSpec(memory_space=pl.ANY),
        pl.BlockSpec((tc, d), lambda i, pos: (i, 0)),
        pl.BlockSpec((None, 1, d), lambda i, pos: ((i * tc) // group_rows, 0, 0)),
        pl.BlockSpec((tc, TOP_K), lambda i, pos: (tok_base // tc + i, 0)),
    ]
    args = [pos4, ys, h, gate, w4]
    if final:
        in_specs.append(pl.BlockSpec((1, d), lambda i, pos: (0, 0)))
        args.append(final_w.reshape(1, d))
    grid_spec = pltpu.PrefetchScalarGridSpec(
        num_scalar_prefetch=1,
        grid=(m // tc,),
        in_specs=in_specs,
        out_specs=pl.BlockSpec((tc, d), lambda i, pos: (i, 0)),
        scratch_shapes=[pltpu.VMEM((2, TOP_K * tc, d), F32), pltpu.SemaphoreType.DMA((2,))],
    )
    return pl.pallas_call(
        functools.partial(_combine_kernel, tc=tc, d=d, tok_base=tok_base, final=final),
        grid_spec=grid_spec,
        out_shape=jax.ShapeDtypeStruct((m, d), F32),
        compiler_params=_params("arbitrary"),
        name="moe_combine",
    )(*args)


def _moe(h_lat, h_ctx, wm, sh, gate, rw, rb, w1, b1, w2, b2, *, s_len, final_w=None):
    ml = h_lat.shape[0]
    mc = 0 if h_ctx is None else h_ctx.shape[0]
    t = ml + mc
    n_exp = rw.shape[1]
    nb = ml // s_len
    tm_r = _tile(math.gcd(ml, mc) if mc else ml, 256)
    n_packed, idx8, w8, r8, cnt = _router(h_lat, h_ctx, wm, sh, rw, rb, group_rows=s_len, tm=tm_r)
    tm = 256
    n_tiles = (TOP_K * t) // tm + n_exp
    tile_expert, n_used, pos4 = _routing_tables(idx8, r8, cnt, tm=tm, n_tiles=n_tiles)
    w4 = w8[:TOP_K].T
    ys = _experts(n_packed, tile_expert, n_used, pos4, w1, b1, w2, b2, tm=tm, n_tiles=n_tiles)
    new_lat = _combine(ys, pos4, w4, h_lat, gate[:nb], group_rows=s_len, tok_base=0, final_w=final_w)
    new_ctx = None
    if mc:
        new_ctx = _combine(ys, pos4, w4, h_ctx, gate[nb:nb + 1], group_rows=mc, tok_base=ml)
    return new_lat, new_ctx


def _axial_tables(s_len, hd):
    half = hd // 2
    inv = 1.0 / (ROPE_BASE ** (jnp.arange(0, half, 2, dtype=F32) / half))
    t = jnp.arange(s_len)
    rows = (t // GRID_W).astype(F32)[:, None] * inv[None, :]
    cols = (t % GRID_W).astype(F32)[:, None] * inv[None, :]
    cos = jnp.concatenate([jnp.cos(rows), jnp.cos(rows), jnp.cos(cols), jnp.cos(cols)], axis=-1)
    sin = jnp.concatenate([-jnp.sin(rows), jnp.sin(rows), -jnp.sin(cols), jnp.sin(cols)], axis=-1)
    return cos, sin


def _retention_tables(s_len, dk):
    theta = 1.0 / (ROPE_BASE ** jnp.linspace(0.0, 1.0, dk // 2, dtype=F32))
    ang = jnp.arange(s_len, dtype=F32)[:, None] * theta[None, :]
    return jnp.cos(ang), jnp.sin(ang)


def kernel(x, c, ctx, c_ctx, mod_w, mod_b, norm1_w, norm2_w, attn_w_qkv, attn_w_o, attn_sinks, ret_w_in,
           ret_w_out, ret_a_fwd, ret_a_bwd, router_w, router_b, exp_w1, exp_b1, exp_w2, exp_b2, final_norm_w):
    batch, s_len, d = x.shape
    c_len = ctx.shape[1]
    depth = mod_w.shape[0]
    assert depth == 2 and attn_w_qkv.shape[0] == 1 and ret_w_in.shape[0] == 1
    n_heads = attn_sinks.shape[1]
    hd = d // n_heads
    n_kv = (attn_w_qkv.shape[2] // hd - n_heads) // 2
    r_heads = ret_a_fwd.shape[1]
    dk = d // r_heads
    dv = 2 * dk
    qd, vd = r_heads * dk, r_heads * dv
    assert hd == LANES and dk == 2 * LANES
    ml, mc = batch * s_len, batch * c_len

    n_rows = -(-(batch + 1) // 8) * 8
    cvec = jnp.zeros((n_rows, d), F32).at[:batch].set(c).at[batch].set(c_ctx)
    mods = _modulation(cvec, mod_w, mod_b)

    def mod_tables(i):
        sh1, sc1, g1, sh2, sc2, g2 = [a[:batch + 1, None, :] for a in jnp.split(mods[i], 6, axis=-1)]
        wm1 = norm1_w[i][None, None, :] * (1.0 + sc1)
        wm2 = norm2_w[i][None, None, :] * (1.0 + sc2)
        return wm1, sh1, g1, wm2, sh2, g2

    x2 = x.reshape(ml, d)
    ctx2 = ctx.reshape(mc, d)

    wm1, sh1, g1, wm2, sh2, g2 = mod_tables(0)
    w_qkv = attn_w_qkv[0].astype(BF16)
    nq = n_heads * hd
    qkv_lat = _norm_matmul(x2, wm1[:batch], sh1[:batch], w_qkv, group_rows=s_len, out_dtype=BF16,
                           rot="axial", tabs=_axial_tables(s_len, hd), rot_cols=nq + n_kv * hd,
                           name="qkv_latent")
    qkv_ctx = _norm_matmul(ctx2, wm1[batch:], sh1[batch:], w_qkv, group_rows=mc, out_dtype=BF16,
                           name="qkv_context")
    o_lat, o_ctx = _attention(qkv_lat, qkv_ctx, attn_sinks[0], batch=batch, s_len=s_len, c_len=c_len,
                              n_heads=n_heads, n_kv=n_kv, hd=hd)
    w_o = attn_w_o[0].astype(BF16)
    h_lat = _matmul_residual(o_lat, w_o, x2, g1[:batch], group_rows=s_len, name="attn_out_latent")
    h_ctx = _matmul_residual(o_ctx, w_o, ctx2, g1[batch:], group_rows=mc, name="attn_out_context")
    h_lat, h_ctx = _moe(h_lat, h_ctx, wm2, sh2, g2, router_w[0], router_b[0], exp_w1[0].astype(BF16),
                        exp_b1[0], exp_w2[0].astype(BF16), exp_b2[0], s_len=s_len)

    wm1, sh1, g1, wm2, sh2, g2 = mod_tables(1)
    w_in = ret_w_in[0].astype(BF16)
    k_scale = dk ** -0.5
    p_lat = _norm_matmul(h_lat, wm1[:batch], sh1[:batch], w_in, group_rows=s_len, out_dtype=F32,
                         rot="half", tabs=_retention_tables(s_len, dk), rot_cols=2 * qd,
                         k_lo=qd, k_hi=2 * qd, k_scale=k_scale, name="ret_in_latent")
    p_ctx = _norm_matmul(h_ctx, wm1[batch:], sh1[batch:], w_in, group_rows=mc, out_dtype=F32,
                         col_off=qd, n_out=qd + vd, k_lo=qd, k_hi=2 * qd, k_scale=k_scale,
                         name="ret_in_context")
    lg = jnp.stack([jnp.log1p(-jnp.exp(ret_a_fwd[0].astype(F32))), jnp.log1p(-jnp.exp(ret_a_bwd[0].astype(F32)))])
    r_lat = _retention(p_lat, p_ctx, lg, batch=batch, s_len=s_len, c_len=c_len, n_heads=r_heads, dk=dk, dv=dv)
    h_lat = _matmul_residual(r_lat, ret_w_out[0].astype(BF16), h_lat, g1[:batch], group_rows=s_len,
                             name="ret_out_latent")
    out, _ = _moe(h_lat, None, wm2, sh2, g2, router_w[1], router_b[1], exp_w1[1].astype(BF16), exp_b1[1],
                  exp_w2[1].astype(BF16), exp_b2[1], s_len=s_len, final_w=final_norm_w)
    return out.reshape(batch, s_len, d)
```

```python
import functools
import math

import jax
import jax.numpy as jnp
from jax import lax
from jax.experimental import pallas as pl
from jax.experimental.pallas import tpu as pltpu

F32 = jnp.float32
BF16 = jnp.bfloat16
I32 = jnp.int32

GRID_W = 64
ATTN_WINDOW = 128
ATTN_BLOCK = 128
ATTN_SOFTMAX_ROWS = 32
RET_CHUNK = 128
ROPE_BASE = 10000.0
TOP_K = 4
SWIGLU_LIMIT = 7.0
SWIGLU_ALPHA = 1.702
NORM_EPS = 1e-6

LANES = 128
NORM_ROWS = 16
DMA_UNROLL = 32
MM_WEIGHT_ELEMS = 4 * 1024 * 1024
MM_OUT_BLOCK_BYTES = 2 * 1024 * 1024
VMEM_LIMIT = 58 * 1024 * 1024

_NT = (((1,), (1,)), ((), ()))
_TN = (((0,), (0,)), ((), ()))


def _params(*sem):
    return pltpu.CompilerParams(dimension_semantics=sem, vmem_limit_bytes=VMEM_LIMIT)


def _tile(dim, pref, *also):
    t = min(dim, pref)
    while any(e % t for e in (dim,) + also):
        t //= 2
    assert t >= 8, (dim, pref, also)
    return t


def _mod_kernel(s_ref, w_ref, b_ref, o_ref):
    s = s_ref[...]
    s = s * jax.nn.sigmoid(s)
    o_ref[...] = jnp.dot(s.astype(BF16), w_ref[...].astype(BF16), preferred_element_type=F32) + b_ref[...]


def _modulation(cvec, mod_w, mod_b):
    nl, d, n = mod_w.shape
    r = cvec.shape[0]
    tn = _tile(n, 512)
    return pl.pallas_call(
        _mod_kernel,
        grid=(nl, n // tn),
        in_specs=[
            pl.BlockSpec((r, d), lambda l, j: (0, 0)),
            pl.BlockSpec((None, d, tn), lambda l, j: (l, 0, j)),
            pl.BlockSpec((None, 1, tn), lambda l, j: (l, 0, j)),
        ],
        out_specs=pl.BlockSpec((None, r, tn), lambda l, j: (l, 0, j)),
        out_shape=jax.ShapeDtypeStruct((nl, r, n), F32),
        compiler_params=_params("arbitrary", "arbitrary"),
        name="modulation",
    )(cvec, mod_w, mod_b.reshape(nl, 1, n))


def _norm_rows(x, wm, sh, d):
    ms = jnp.sum(x * x, axis=-1, keepdims=True) * (1.0 / d)
    return x * lax.rsqrt(ms + NORM_EPS) * wm + sh


def _nmm_kernel(*refs, d, tm, tn, col_off, rot, rot_cols, k_lo, k_hi, k_scale):
    if rot is None:
        x_ref, wm_ref, sh_ref, w_ref, o_ref, n_scr = refs
        cos_ref = sin_ref = None
    else:
        x_ref, wm_ref, sh_ref, w_ref, cos_ref, sin_ref, o_ref, n_scr = refs
    j = pl.program_id(1)

    @pl.when(j == 0)
    def _():
        wm = wm_ref[...]
        sh = sh_ref[...]

        def body(r, c):
            rows = pl.ds(pl.multiple_of(r * NORM_ROWS, NORM_ROWS), NORM_ROWS)
            n_scr[rows, :] = _norm_rows(x_ref[rows, :], wm, sh, d).astype(BF16)
            return c

        lax.fori_loop(0, tm // NORM_ROWS, body, 0)

    acc = jnp.dot(n_scr[...], w_ref[...], preferred_element_type=F32)
    col0 = j * tn + col_off
    if k_scale is not None:
        acc = acc * jnp.where((col0 >= k_lo) & (col0 < k_hi), k_scale, 1.0).astype(F32)

    if rot is None:
        o_ref[...] = acc.astype(o_ref.dtype)
        return

    @pl.when(col0 >= rot_cols)
    def _():
        o_ref[...] = acc.astype(o_ref.dtype)

    @pl.when(col0 < rot_cols)
    def _():
        cos = cos_ref[...]
        sin = sin_ref[...]
        if rot == "axial":
            lane = lax.broadcasted_iota(I32, (tm, LANES), 1)
            first = (lane & 32) == 0
            for hh in range(tn // LANES):
                a = acc[:, hh * LANES:(hh + 1) * LANES]
                partner = jnp.where(first, pltpu.roll(a, LANES - 32, 1), pltpu.roll(a, 32, 1))
                o_ref[:, hh * LANES:(hh + 1) * LANES] = (a * cos + partner * sin).astype(o_ref.dtype)
        else:
            for hh in range(tn // (2 * LANES)):
                c0 = hh * 2 * LANES
                x1 = acc[:, c0:c0 + LANES]
                x2 = acc[:, c0 + LANES:c0 + 2 * LANES]
                o_ref[:, c0:c0 + LANES] = (x1 * cos - x2 * sin).astype(o_ref.dtype)
                o_ref[:, c0 + LANES:c0 + 2 * LANES] = (x1 * sin + x2 * cos).astype(o_ref.dtype)


def _norm_matmul(h, wm, sh, w, *, group_rows, out_dtype, col_off=0, n_out=None, rot=None, tabs=None,
                 rot_cols=0, k_lo=0, k_hi=0, k_scale=None, name="norm_matmul"):
    m, d = h.shape
    n_out = w.shape[1] - col_off if n_out is None else n_out
    tm = _tile(m, 1024, group_rows)
    tn = min(MM_WEIGHT_ELEMS // d, MM_OUT_BLOCK_BYTES // (tm * jnp.dtype(out_dtype).itemsize))
    tn = _tile(n_out, tn, col_off, rot_cols, k_lo, k_hi)
    in_specs = [
        pl.BlockSpec((tm, d), lambda i, j: (i, 0), pipeline_mode=pl.Buffered(1)),
        pl.BlockSpec((None, 1, d), lambda i, j: ((i * tm) // group_rows, 0, 0)),
        pl.BlockSpec((None, 1, d), lambda i, j: ((i * tm) // group_rows, 0, 0)),
        pl.BlockSpec((d, tn), lambda i, j: (0, j + col_off // tn)),
    ]
    args = [h, wm, sh, w]
    if rot is not None:
        cos, sin = tabs
        s_pos = cos.shape[0]
        assert s_pos % tm == 0 and rot_cols % tn == 0
        nper = s_pos // tm
        in_specs += [pl.BlockSpec((tm, LANES), lambda i, j: (i % nper, 0))] * 2
        args += [cos, sin]
    kern = functools.partial(_nmm_kernel, d=d, tm=tm, tn=tn, col_off=col_off, rot=rot, rot_cols=rot_cols,
                             k_lo=k_lo, k_hi=k_hi, k_scale=k_scale)
    return pl.pallas_call(
        kern,
        grid=(m // tm, n_out // tn),
        in_specs=in_specs,
        out_specs=pl.BlockSpec((tm, tn), lambda i, j: (i, j)),
        out_shape=jax.ShapeDtypeStruct((m, n_out), out_dtype),
        scratch_shapes=[pltpu.VMEM((tm, d), BF16)],
        compiler_params=_params("arbitrary", "arbitrary"),
        name=name,
    )(*args)


def _mmres_kernel(a_ref, w_ref, h_ref, g_ref, o_ref):
    acc = jnp.dot(a_ref[...], w_ref[...], preferred_element_type=F32)
    o_ref[...] = h_ref[...] + g_ref[...] * acc


def _matmul_residual(a, w, h, gate, *, group_rows, name="matmul_residual"):
    m, k = a.shape
    n = w.shape[1]
    tm = _tile(m, 1024, group_rows)
    tn = _tile(n, max(LANES, MM_WEIGHT_ELEMS // k))
    return pl.pallas_call(
        _mmres_kernel,
        grid=(m // tm, n // tn),
        in_specs=[
            pl.BlockSpec((tm, k), lambda i, j: (i, 0), pipeline_mode=pl.Buffered(1)),
            pl.BlockSpec((k, tn), lambda i, j: (0, j)),
            pl.BlockSpec((tm, tn), lambda i, j: (i, j)),
            pl.BlockSpec((None, 1, tn), lambda i, j: ((i * tm) // group_rows, 0, j)),
        ],
        out_specs=pl.BlockSpec((tm, tn), lambda i, j: (i, j)),
        out_shape=jax.ShapeDtypeStruct((m, n), F32),
        compiler_params=_params("arbitrary", "arbitrary"),
        name=name,
    )(a, w, h, gate)


def _softmax_pv(s_parts, v_parts, sink_col):
    m = sink_col
    for s in s_parts:
        m = jnp.maximum(m, jnp.max(s, axis=-1, keepdims=True))
    es = [jnp.exp(s - m) for s in s_parts]
    den = jnp.exp(sink_col - m)
    for e in es:
        den = den + jnp.sum(e, axis=-1, keepdims=True)
    inv = 1.0 / den
    o = None
    for e, v in zip(es, v_parts):
        t = jnp.dot((e * inv).astype(BF16), v, preferred_element_type=F32)
        o = t if o is None else o + t
    return o


def _sink_column(sink_ref, kv, g_heads, rows):
    return jnp.concatenate([jnp.full((rows, 1), sink_ref[kv * g_heads + g], F32) for g in range(g_heads)], axis=0)


def _attn_lat_kernel(sink_ref, q_ref, k_ref, v_ref, kc_ref, vc_ref, o_ref, s_scr, p_scr, bias_scr, *,
                     s_len, c_len, tq, g_heads, hd, scale):
    kv = pl.program_id(1)
    qi = pl.program_id(2)
    blk = ATTN_BLOCK
    win = 3 * blk
    nblk = tq // blk
    rc = ATTN_SOFTMAX_ROWS
    kc = kc_ref[...]
    vc = vc_ref[...]
    rows = lax.broadcasted_iota(I32, (blk, win), 0)
    cols = lax.broadcasted_iota(I32, (blk, win), 1)
    rel = rows - cols + ATTN_WINDOW

    def body(bi, c):
        n = qi * nblk + bi
        start = pl.multiple_of(jnp.clip((n - 1) * blk, 0, s_len - win), blk)
        dist = rel + (n * blk - start)
        bias_scr[...] = jnp.where((dist >= 0) & (dist <= 2 * ATTN_WINDOW), 0.0, -jnp.inf)
        r0 = pl.multiple_of(bi * blk, blk)
        q4 = jnp.concatenate([q_ref[pl.ds(r0, blk), g * hd:(g + 1) * hd] for g in range(g_heads)], axis=0)
        s_scr[:, 0:win] = lax.dot_general(q4, k_ref[pl.ds(start, win), :], _NT, preferred_element_type=F32)
        s_scr[:, win:win + c_len] = lax.dot_general(q4, kc, _NT, preferred_element_type=F32)

        def softmax_rows(ci):
            rr = pl.ds(ci * rc, rc)
            qrow = pl.ds((ci * rc) % blk, rc)
            sink = sink_ref[kv * g_heads + (ci * rc) // blk]
            s_loc = s_scr[rr, 0:win] * scale + bias_scr[qrow, :]
            s_ctx = s_scr[rr, win:win + c_len] * scale
            m = jnp.maximum(jnp.maximum(jnp.max(s_loc, axis=-1, keepdims=True),
                                        jnp.max(s_ctx, axis=-1, keepdims=True)), sink)
            e_loc = jnp.exp(s_loc - m)
            e_ctx = jnp.exp(s_ctx - m)
            den = (jnp.sum(e_loc, axis=-1, keepdims=True) + jnp.sum(e_ctx, axis=-1, keepdims=True)
                   + jnp.exp(sink - m))
            inv = 1.0 / den
            p_scr[rr, 0:win] = (e_loc * inv).astype(p_scr.dtype)
            p_scr[rr, win:win + c_len] = (e_ctx * inv).astype(p_scr.dtype)

        for ci in range((g_heads * blk) // rc):
            softmax_rows(ci)
        o = (jnp.dot(p_scr[:, 0:win], v_ref[pl.ds(start, win), :], preferred_element_type=F32)
             + jnp.dot(p_scr[:, win:win + c_len], vc, preferred_element_type=F32))
        for g in range(g_heads):
            o_ref[pl.ds(r0, blk), g * hd:(g + 1) * hd] = o[g * blk:(g + 1) * blk].astype(o_ref.dtype)
        return c

    lax.fori_loop(0, nblk, body, 0)


def _attn_ctx_kernel(sink_ref, q_ref, kc_ref, vc_ref, o_ref, *, c_len, g_heads, hd, scale):
    kv = pl.program_id(1)
    sink_col = _sink_column(sink_ref, kv, g_heads, c_len)
    q4 = jnp.concatenate([q_ref[:, g * hd:(g + 1) * hd] for g in range(g_heads)], axis=0)
    kc = kc_ref[...]
    s = lax.dot_general(q4, kc, _NT, preferred_element_type=F32) * scale
    o = _softmax_pv([s], [vc_ref[...]], sink_col)
    for g in range(g_heads):
        o_ref[:, g * hd:(g + 1) * hd] = o[g * c_len:(g + 1) * c_len].astype(o_ref.dtype)


def _attention(qkv_lat, qkv_ctx, sinks, *, batch, s_len, c_len, n_heads, n_kv, hd):
    g_heads = n_heads // n_kv
    nq = n_heads * hd
    scale = hd ** -0.5
    tq = _tile(s_len, 1024)
    nqt = s_len // tq
    qw = g_heads * hd
    kcol = nq // hd
    vcol = kcol + n_kv
    smem = pl.BlockSpec(memory_space=pltpu.SMEM)
    lat = pl.pallas_call(
        functools.partial(_attn_lat_kernel, s_len=s_len, c_len=c_len, tq=tq, g_heads=g_heads, hd=hd, scale=scale),
        grid=(batch, n_kv, nqt),
        scratch_shapes=[
            pltpu.VMEM((g_heads * ATTN_BLOCK, 3 * ATTN_BLOCK + c_len), F32),
            pltpu.VMEM((g_heads * ATTN_BLOCK, 3 * ATTN_BLOCK + c_len), BF16),
            pltpu.VMEM((ATTN_BLOCK, 3 * ATTN_BLOCK), F32),
        ],
        in_specs=[
            smem,
            pl.BlockSpec((tq, qw), lambda b, h, i: (b * nqt + i, h)),
            pl.BlockSpec((s_len, hd), lambda b, h, i: (b, kcol + h)),
            pl.BlockSpec((s_len, hd), lambda b, h, i: (b, vcol + h)),
            pl.BlockSpec((c_len, hd), lambda b, h, i: (b, kcol + h)),
            pl.BlockSpec((c_len, hd), lambda b, h, i: (b, vcol + h)),
        ],
        out_specs=pl.BlockSpec((tq, qw), lambda b, h, i: (b * nqt + i, h)),
        out_shape=jax.ShapeDtypeStruct((batch * s_len, nq), BF16),
        compiler_params=_params("arbitrary", "arbitrary", "arbitrary"),
        name="attention_latent",
    )(sinks, qkv_lat, qkv_lat, qkv_lat, qkv_ctx, qkv_ctx)
    ctx = pl.pallas_call(
        functools.partial(_attn_ctx_kernel, c_len=c_len, g_heads=g_heads, hd=hd, scale=scale),
        grid=(batch, n_kv),
        in_specs=[
            smem,
            pl.BlockSpec((c_len, qw), lambda b, h: (b, h)),
            pl.BlockSpec((c_len, hd), lambda b, h: (b, kcol + h)),
            pl.BlockSpec((c_len, hd), lambda b, h: (b, vcol + h)),
        ],
        out_specs=pl.BlockSpec((c_len, qw), lambda b, h: (b, h)),
        out_shape=jax.ShapeDtypeStruct((batch * c_len, nq), BF16),
        compiler_params=_params("arbitrary", "arbitrary"),
        name="attention_context",
    )(sinks, qkv_ctx, qkv_ctx, qkv_ctx)
    return lat, ctx


def _ret_kernel(lg_ref, q_ref, k_ref, v_ref, g_ref, kc_ref, vc_ref, o_ref, s_scr, of_scr, *,
                n_seg, ts, c_len, dv):
    h = pl.program_id(1)
    j = pl.program_id(2)
    ch = RET_CHUNK
    nch = ts // ch

    def sweep(bwd):
        seg = 2 * n_seg - 1 - j if bwd else j
        lg = lg_ref[1 if bwd else 0, h]
        qi = lax.broadcasted_iota(I32, (ch, ch), 0)
        ki = lax.broadcasted_iota(I32, (ch, ch), 1)
        diff = ((ki - qi) if bwd else (qi - ki)).astype(F32)
        d_mask = jnp.exp(jnp.where(diff >= 0, diff * lg, -jnp.inf))
        pos = lax.broadcasted_iota(I32, (ch, 1), 0).astype(F32)
        fpos = ch - 1.0 - pos if bwd else pos
        xi = jnp.exp((fpos + 1.0) * lg)
        zeta = jnp.exp((ch - 1.0 - fpos) * lg)
        chunk_decay = jnp.exp(jnp.full((1, 1), ch, F32) * lg)

        @pl.when(j == (n_seg if bwd else 0))
        def _():
            cpos = lax.broadcasted_iota(I32, (c_len, 1), 0).astype(F32)
            w = jnp.exp((cpos if bwd else c_len - 1.0 - cpos) * lg)
            s_scr[...] = lax.dot_general((kc_ref[...] * w).astype(BF16), vc_ref[...].astype(BF16), _TN,
                                         preferred_element_type=F32)

        def chunk(ci, c):
            cc = nch - 1 - ci if bwd else ci
            r0 = pl.multiple_of(cc * ch, ch)
            rows = pl.ds(r0, ch)
            q = q_ref[rows, :].astype(BF16)
            kf = k_ref[rows, :]
            v = v_ref[rows, :].astype(BF16)
            s = s_scr[...]
            inner = lax.dot_general(q, kf.astype(BF16), _NT, preferred_element_type=F32) * d_mask
            o = jnp.dot(inner.astype(BF16), v, preferred_element_type=F32)
            o = o + jnp.dot(q, s.astype(BF16), preferred_element_type=F32) * xi
            s_scr[...] = chunk_decay * s + lax.dot_general((kf * zeta).astype(BF16), v, _TN,
                                                           preferred_element_type=F32)
            grow = pl.ds(pl.multiple_of(seg * ts + r0, ch), ch)
            if bwd:
                t = of_scr[grow, :] + o
                t = t * lax.rsqrt(jnp.sum(t * t, axis=-1, keepdims=True) * (1.0 / dv) + NORM_EPS)
                g = g_ref[rows, :]
                o_ref[rows, :] = (g * jax.nn.sigmoid(g) * t).astype(o_ref.dtype)
            else:
                of_scr[grow, :] = o
            return c

        lax.fori_loop(0, nch, chunk, 0, unroll=4)

    pl.when(j < n_seg)(lambda: sweep(False))
    pl.when(j >= n_seg)(lambda: sweep(True))


def _retention(p_lat, p_ctx, lg, *, batch, s_len, c_len, n_heads, dk, dv):
    qd = n_heads * dk
    vd = n_heads * dv
    ts = _tile(s_len, 1024)
    n_seg = s_len // ts

    def seg_of(j):
        return jnp.where(j >= n_seg, 2 * n_seg - 1 - j, j)

    def seg_bwd(j):
        return jnp.where(j >= n_seg, 2 * n_seg - 1 - j, n_seg - 1)

    kq = qd // dk
    kv_ = (2 * qd) // dv
    kg = (2 * qd + vd) // dv
    return pl.pallas_call(
        functools.partial(_ret_kernel, n_seg=n_seg, ts=ts, c_len=c_len, dv=dv),
        grid=(batch, n_heads, 2 * n_seg),
        in_specs=[
            pl.BlockSpec(memory_space=pltpu.SMEM),
            pl.BlockSpec((ts, dk), lambda b, h, j: (b * n_seg + seg_of(j), h)),
            pl.BlockSpec((ts, dk), lambda b, h, j: (b * n_seg + seg_of(j), kq + h)),
            pl.BlockSpec((ts, dv), lambda b, h, j: (b * n_seg + seg_of(j), kv_ + h)),
            pl.BlockSpec((ts, dv), lambda b, h, j: (b * n_seg + seg_bwd(j), kg + h)),
            pl.BlockSpec((c_len, dk), lambda b, h, j: (b, h)),
            pl.BlockSpec((c_len, dv), lambda b, h, j: (b, qd // dv + h)),
        ],
        out_specs=pl.BlockSpec((ts, dv), lambda b, h, j: (b * n_seg + seg_bwd(j), h)),
        out_shape=jax.ShapeDtypeStruct((batch * s_len, vd), BF16),
        scratch_shapes=[pltpu.VMEM((dk, dv), F32), pltpu.VMEM((s_len, dv), F32)],
        compiler_params=_params("arbitrary", "arbitrary", "arbitrary"),
        name="retention",
    )(lg, p_lat, p_lat, p_lat, p_lat, p_ctx, p_ctx)


def _router_kernel(*refs, n_src, nl_tiles, tm, d, n_exp):
    if n_src == 2:
        hl_ref, hc_ref, wm_ref, sh_ref, w_ref, b_ref, n_ref, idx_ref, w8_ref, r8_ref, cnt_ref, carry, nhi, nlo = refs
    else:
        hl_ref, wm_ref, sh_ref, w_ref, b_ref, n_ref, idx_ref, w8_ref, r8_ref, cnt_ref, carry, nhi, nlo = refs
        hc_ref = None
    i = pl.program_id(0)
    dh = d // 2

    @pl.when(i == 0)
    def _():
        carry[...] = jnp.zeros_like(carry)

    wm = wm_ref[...]
    sh = sh_ref[...]

    def fill(src_ref):
        def body(r, c):
            rows = pl.ds(pl.multiple_of(r * NORM_ROWS, NORM_ROWS), NORM_ROWS)
            n = _norm_rows(src_ref[rows, :], wm, sh, d)
            hi = n.astype(BF16)
            hi32 = hi.astype(F32)
            bits = lax.bitcast_convert_type(hi32, jnp.uint32)
            n_ref[rows, :] = (bits[:, :dh] >> 16) | (bits[:, dh:] & jnp.uint32(0xFFFF0000))
            nhi[rows, :] = hi
            nlo[rows, :] = (n - hi32).astype(BF16)
            return c

        lax.fori_loop(0, tm // NORM_ROWS, body, 0)

    if n_src == 2:
        pl.when(i < nl_tiles)(lambda: fill(hl_ref))
        pl.when(i >= nl_tiles)(lambda: fill(hc_ref))
    else:
        fill(hl_ref)

    w_hi = w_ref[0:n_exp, :]
    w_lo = w_ref[n_exp:2 * n_exp, :]
    x_hi = nhi[...]
    logits = (lax.dot_general(w_hi, x_hi, _NT, preferred_element_type=F32)
              + lax.dot_general(w_lo, x_hi, _NT, preferred_element_type=F32)
              + lax.dot_general(w_hi, nlo[...], _NT, preferred_element_type=F32)
              + b_ref[...])

    eidx = lax.broadcasted_iota(I32, (n_exp, tm), 0)
    row8 = lax.broadcasted_iota(I32, (8, tm), 0)
    rest = logits
    sel = jnp.zeros((n_exp, tm), jnp.bool_)
    picks = []
    idx8 = jnp.zeros((8, tm), I32)
    top1 = None
    for k in range(TOP_K):
        mx = jnp.max(rest, axis=0, keepdims=True)
        if k == 0:
            top1 = mx
        am = jnp.min(jnp.where(rest == mx, eidx, n_exp), axis=0, keepdims=True)
        pick = eidx == am
        picks.append(pick)
        sel = sel | pick
        idx8 = jnp.where(row8 == k, am, idx8)
        rest = jnp.where(pick, -jnp.inf, rest)
    e = jnp.where(sel, jnp.exp(logits - top1), 0.0)
    gates = e / jnp.sum(e, axis=0, keepdims=True)

    t0 = lax.broadcasted_iota(I32, (tm, tm), 0)
    t1 = lax.broadcasted_iota(I32, (tm, tm), 1)
    before = jnp.where(t0 < t1, 1.0, 0.0).astype(BF16)
    self_f = jnp.where(sel, 1.0, 0.0)
    rank = jnp.dot(self_f.astype(BF16), before, preferred_element_type=F32) + carry[:, 0:1]
    w8 = jnp.zeros((8, tm), F32)
    r8 = jnp.zeros((8, tm), F32)
    for k, pick in enumerate(picks):
        w8 = jnp.where(row8 == k, jnp.sum(jnp.where(pick, gates, 0.0), axis=0, keepdims=True), w8)
        r8 = jnp.where(row8 == k, jnp.sum(jnp.where(pick, rank, 0.0), axis=0, keepdims=True), r8)
    idx_ref[...] = idx8
    w8_ref[...] = w8
    r8_ref[...] = r8.astype(I32)
    new_carry = carry[...] + jnp.sum(self_f, axis=1, keepdims=True)
    carry[...] = new_carry
    cnt_ref[...] = new_carry


def _router(h_lat, h_ctx, wm, sh, rw, rb, *, group_rows, tm):
    ml, d = h_lat.shape
    n_exp = rw.shape[1]
    mc = 0 if h_ctx is None else h_ctx.shape[0]
    t = ml + mc
    assert ml % tm == 0 and mc % tm == 0 and group_rows % tm == 0
    nl_tiles = ml // tm
    n_groups_lat = ml // group_rows
    rw_t = rw.T
    w_hi = rw_t.astype(BF16)
    w_lo = (rw_t - w_hi.astype(F32)).astype(BF16)
    w3 = jnp.concatenate([w_hi, w_lo], axis=0)

    def grp(i):
        return jnp.where(i < nl_tiles, (i * tm) // group_rows, n_groups_lat)

    in_specs = [pl.BlockSpec((tm, d), lambda i: (jnp.minimum(i, nl_tiles - 1), 0))]
    args = [h_lat]
    if mc:
        in_specs.append(pl.BlockSpec((tm, d), lambda i: (jnp.maximum(i - nl_tiles, 0), 0)))
        args.append(h_ctx)
    in_specs += [
        pl.BlockSpec((None, 1, d), lambda i: (grp(i), 0, 0)),
        pl.BlockSpec((None, 1, d), lambda i: (grp(i), 0, 0)),
        pl.BlockSpec((2 * n_exp, d), lambda i: (0, 0)),
        pl.BlockSpec((n_exp, 1), lambda i: (0, 0)),
    ]
    args += [wm, sh, w3, rb.reshape(n_exp, 1)]
    return pl.pallas_call(
        functools.partial(_router_kernel, n_src=2 if mc else 1, nl_tiles=nl_tiles, tm=tm, d=d, n_exp=n_exp),
        grid=(t // tm,),
        in_specs=in_specs,
        out_specs=[
            pl.BlockSpec((tm, d // 2), lambda i: (i, 0)),
            pl.BlockSpec((8, tm), lambda i: (0, i)),
            pl.BlockSpec((8, tm), lambda i: (0, i)),
            pl.BlockSpec((8, tm), lambda i: (0, i)),
            pl.BlockSpec((n_exp, LANES), lambda i: (0, 0)),
        ],
        out_shape=[
            jax.ShapeDtypeStruct((t, d // 2), jnp.uint32),
            jax.ShapeDtypeStruct((8, t), I32),
            jax.ShapeDtypeStruct((8, t), F32),
            jax.ShapeDtypeStruct((8, t), I32),
            jax.ShapeDtypeStruct((n_exp, LANES), F32),
        ],
        scratch_shapes=[pltpu.VMEM((n_exp, LANES), F32), pltpu.VMEM((tm, d), BF16), pltpu.VMEM((tm, d), BF16)],
        compiler_params=_params("arbitrary"),
        name="moe_router",
    )(*args)


def _routing_tables(idx8, r8, cnt, *, tm, n_tiles):
    n_exp = cnt.shape[0]
    c = cnt[:, 0].astype(I32)
    pc = ((c + tm - 1) // tm) * tm
    cum = jnp.cumsum(pc)
    off = cum - pc
    idx4 = idx8[:TOP_K]
    off4 = jnp.sum(jnp.where(idx4[:, :, None] == jnp.arange(n_exp, dtype=I32), off, 0), axis=-1)
    pos4 = (off4 + r8[:TOP_K]).T.reshape(-1)
    extents = jnp.stack([cum[-1] // tm, jnp.asarray(n_tiles * tm, I32), jnp.asarray(idx8.shape[1], I32)]).astype(I32)
    tile_start = jnp.arange(n_tiles, dtype=I32) * tm
    tile_expert = jnp.minimum(jnp.sum(tile_start[:, None] >= cum[None, :], axis=1), n_exp - 1).astype(I32)
    return tile_expert, extents, pos4


def _moe_kernel(te_ref, nu_ref, pos_ref, n_hbm, w1_ref, b1_ref, w2_ref, b2_ref, o_ref, tok_ref, xbuf, sem, *,
                tm, f_dim, dh):
    i = pl.program_id(0)
    n_used = nu_ref[0]
    slot = i % 2

    def row_copy(tile, r, sl):
        tok = tok_ref[tile * tm + r]
        return pltpu.make_async_copy(n_hbm.at[pl.ds(tok, 1), :], xbuf.at[sl, pl.ds(r, 1), :], sem.at[sl])

    def wait_rows(sl):
        pltpu.make_async_copy(n_hbm.at[pl.ds(0, tm), :], xbuf.at[sl], sem.at[sl]).wait()

    def start_rows(tile, sl):
        def body(rb, c):
            for u in range(DMA_UNROLL):
                row_copy(tile, rb * DMA_UNROLL + u, sl).start()
            return c

        lax.fori_loop(0, tm // DMA_UNROLL, body, 0)

    @pl.when(i == 0)
    def _():
        def zero(jb, c):
            for u in range(DMA_UNROLL):
                tok_ref[jb * DMA_UNROLL + u] = 0
            return c

        lax.fori_loop(0, nu_ref[1] // DMA_UNROLL, zero, 0)

        def invert(t, c):
            for k in range(TOP_K):
                tok_ref[pos_ref[t * TOP_K + k]] = t
            return c

        lax.fori_loop(0, nu_ref[2], invert, 0)
        start_rows(0, 0)

    def step(sl):
        nxt = jnp.minimum(i + 1, n_used - 1)
        for r in range(tm):
            row_copy(nxt, r, 1 - sl).start()
        wait_rows(sl)
        words = xbuf[sl]
        x_lo = lax.bitcast_convert_type(words << 16, F32).astype(BF16)
        x_hi = lax.bitcast_convert_type(words & jnp.uint32(0xFFFF0000), F32).astype(BF16)
        gu = (jnp.dot(x_lo, w1_ref[0:dh, :], preferred_element_type=F32)
              + jnp.dot(x_hi, w1_ref[dh:2 * dh, :], preferred_element_type=F32) + b1_ref[...])
        gate = jnp.minimum(gu[:, :f_dim], SWIGLU_LIMIT)
        up = jnp.clip(gu[:, f_dim:], -SWIGLU_LIMIT, SWIGLU_LIMIT)
        act = (up + 1.0) * (gate * jax.nn.sigmoid(SWIGLU_ALPHA * gate))
        o_ref[...] = jnp.dot(act.astype(BF16), w2_ref[...], preferred_element_type=F32) + b2_ref[...]

        @pl.when(i == n_used - 1)
        def _():
            wait_rows(1 - sl)

    pl.when((i < n_used) & (slot == 0))(lambda: step(0))
    pl.when((i < n_used) & (slot == 1))(lambda: step(1))

    @pl.when(i >= n_used)
    def _():
        o_ref[...] = jnp.zeros_like(o_ref)


def _experts(n_packed, tile_expert, n_used, pos4, w1, b1, w2, b2, *, layer, tm, n_tiles):
    n_layers, n_exp, d, f2 = w1.shape
    f_dim = f2 // 2
    dh = n_packed.shape[1]
    grid_spec = pltpu.PrefetchScalarGridSpec(
        num_scalar_prefetch=3,
        grid=(n_tiles,),
        in_specs=[
            pl.BlockSpec(memory_space=pl.ANY),
            pl.BlockSpec((None, None, d, f2), lambda i, te, nu, pos: (layer, te[i], 0, 0)),
            pl.BlockSpec((None, None, 1, f2), lambda i, te, nu, pos: (layer, te[i], 0, 0)),
            pl.BlockSpec((None, None, f_dim, d), lambda i, te, nu, pos: (layer, te[i], 0, 0)),
            pl.BlockSpec((None, None, 1, d), lambda i, te, nu, pos: (layer, te[i], 0, 0)),
        ],
        out_specs=pl.BlockSpec((tm, d), lambda i, te, nu, pos: (i, 0)),
        scratch_shapes=[pltpu.SMEM((n_tiles * tm,), I32), pltpu.VMEM((2, tm, dh), jnp.uint32),
                        pltpu.SemaphoreType.DMA((2,))],
    )
    return pl.pallas_call(
        functools.partial(_moe_kernel, tm=tm, f_dim=f_dim, dh=dh),
        grid_spec=grid_spec,
        out_shape=jax.ShapeDtypeStruct((n_tiles * tm, d), F32),
        compiler_params=_params("arbitrary"),
        name="moe_experts",
    )(tile_expert, n_used, pos4, n_packed, w1, b1.reshape(n_layers, n_exp, 1, f2), w2,
      b2.reshape(n_layers, n_exp, 1, d))


def _combine_kernel(*refs, tc, d, tok_base, final):
    if final:
        pos_ref, ys_hbm, h_ref, g_ref, w_ref, fw_ref, o_ref, buf, sem = refs
    else:
        pos_ref, ys_hbm, h_ref, g_ref, w_ref, o_ref, buf, sem = refs
        fw_ref = None
    i = pl.program_id(0)
    nt = pl.num_programs(0)
    slot = i % 2

    def start_rows(tile, r0, sl):
        for u in range(NORM_ROWS):
            base = (tok_base + tile * tc + r0 + u) * TOP_K
            for k in range(TOP_K):
                pltpu.make_async_copy(ys_hbm.at[pl.ds(pos_ref[base + k], 1), :],
                                      buf.at[sl, pl.ds(k * tc + r0 + u, 1), :], sem.at[sl]).start()

    def wait_rows(sl):
        pltpu.make_async_copy(ys_hbm.at[pl.ds(0, TOP_K * tc), :], buf.at[sl], sem.at[sl]).wait()

    @pl.when(i == 0)
    def _():
        def first(r, c):
            start_rows(0, r * NORM_ROWS, 0)
            return c

        lax.fori_loop(0, tc // NORM_ROWS, first, 0)

    gate = g_ref[...]
    fw = None if fw_ref is None else fw_ref[...]
    nxt = jnp.minimum(i + 1, nt - 1)

    def step(sl):
        wait_rows(sl)

        def body(r, c):
            r0 = pl.multiple_of(r * NORM_ROWS, NORM_ROWS)
            start_rows(nxt, r0, 1 - sl)
            wk = w_ref[pl.ds(r0, NORM_ROWS), :]
            y = wk[:, 0:1] * buf[sl, pl.ds(r0, NORM_ROWS), :]
            for k in range(1, TOP_K):
                y = y + wk[:, k:k + 1] * buf[sl, pl.ds(k * tc + r0, NORM_ROWS), :]
            out = h_ref[pl.ds(r0, NORM_ROWS), :] + gate * y
            if final:
                ms = jnp.sum(out * out, axis=-1, keepdims=True) * (1.0 / d)
                out = out * lax.rsqrt(ms + NORM_EPS) * fw
            o_ref[pl.ds(r0, NORM_ROWS), :] = out
            return c

        lax.fori_loop(0, tc // NORM_ROWS, body, 0)

        @pl.when(i == nt - 1)
        def _():
            wait_rows(1 - sl)

    pl.when(slot == 0)(lambda: step(0))
    pl.when(slot == 1)(lambda: step(1))


def _combine(ys, pos4, w4, h, gate, *, group_rows, tok_base, final_w=None):
    m, d = h.shape
    tc = _tile(m, 128)
    assert group_rows % tc == 0 and tok_base % tc == 0
    final = final_w is not None
    in_specs = [
        pl.BlockSpec(memory_space=pl.ANY),
        pl.BlockSpec((tc, d), lambda i, pos: (i, 0)),
        pl.BlockSpec((None, 1, d), lambda i, pos: ((i * tc) // group_rows, 0, 0)),
        pl.BlockSpec((tc, TOP_K), lambda i, pos: (tok_base // tc + i, 0)),
    ]
    args = [pos4, ys, h, gate, w4]
    if final:
        in_specs.append(pl.BlockSpec((1, d), lambda i, pos: (0, 0)))
        args.append(final_w.reshape(1, d))
    grid_spec = pltpu.PrefetchScalarGridSpec(
        num_scalar_prefetch=1,
        grid=(m // tc,),
        in_specs=in_specs,
        out_specs=pl.BlockSpec((tc, d), lambda i, pos: (i, 0)),
        scratch_shapes=[pltpu.VMEM((2, TOP_K * tc, d), F32), pltpu.SemaphoreType.DMA((2,))],
    )
    return pl.pallas_call(
        functools.partial(_combine_kernel, tc=tc, d=d, tok_base=tok_base, final=final),
        grid_spec=grid_spec,
        out_shape=jax.ShapeDtypeStruct((m, d), F32),
        compiler_params=_params("arbitrary"),
        name="moe_combine",
    )(*args)


def _moe(h_lat, h_ctx, wm, sh, gate, rw, rb, w1, b1, w2, b2, *, layer, s_len, final_w=None):
    ml = h_lat.shape[0]
    mc = 0 if h_ctx is None else h_ctx.shape[0]
    t = ml + mc
    n_exp = rw.shape[1]
    nb = ml // s_len
    tm_r = _tile(math.gcd(ml, mc) if mc else ml, 256)
    n_packed, idx8, w8, r8, cnt = _router(h_lat, h_ctx, wm, sh, rw, rb, group_rows=s_len, tm=tm_r)
    tm = 256
    n_tiles = (TOP_K * t) // tm + n_exp
    tile_expert, n_used, pos4 = _routing_tables(idx8, r8, cnt, tm=tm, n_tiles=n_tiles)
    w4 = w8[:TOP_K].T
    ys = _experts(n_packed, tile_expert, n_used, pos4, w1, b1, w2, b2, layer=layer, tm=tm, n_tiles=n_tiles)
    new_lat = _combine(ys, pos4, w4, h_lat, gate[:nb], group_rows=s_len, tok_base=0, final_w=final_w)
    new_ctx = None
    if mc:
        new_ctx = _combine(ys, pos4, w4, h_ctx, gate[nb:nb + 1], group_rows=mc, tok_base=ml)
    return new_lat, new_ctx


def _axial_tables(s_len, hd):
    half = hd // 2
    inv = 1.0 / (ROPE_BASE ** (jnp.arange(0, half, 2, dtype=F32) / half))
    t = jnp.arange(s_len)
    rows = (t // GRID_W).astype(F32)[:, None] * inv[None, :]
    cols = (t % GRID_W).astype(F32)[:, None] * inv[None, :]
    cos = jnp.concatenate([jnp.cos(rows), jnp.cos(rows), jnp.cos(cols), jnp.cos(cols)], axis=-1)
    sin = jnp.concatenate([-jnp.sin(rows), jnp.sin(rows), -jnp.sin(cols), jnp.sin(cols)], axis=-1)
    return cos, sin


def _retention_tables(s_len, dk):
    theta = 1.0 / (ROPE_BASE ** jnp.linspace(0.0, 1.0, dk // 2, dtype=F32))
    ang = jnp.arange(s_len, dtype=F32)[:, None] * theta[None, :]
    return jnp.cos(ang), jnp.sin(ang)


def kernel(x, c, ctx, c_ctx, mod_w, mod_b, norm1_w, norm2_w, attn_w_qkv, attn_w_o, attn_sinks, ret_w_in,
           ret_w_out, ret_a_fwd, ret_a_bwd, router_w, router_b, exp_w1, exp_b1, exp_w2, exp_b2, final_norm_w):
    batch, s_len, d = x.shape
    c_len = ctx.shape[1]
    depth = mod_w.shape[0]
    assert depth == 2 and attn_w_qkv.shape[0] == 1 and ret_w_in.shape[0] == 1
    n_heads = attn_sinks.shape[1]
    hd = d // n_heads
    n_kv = (attn_w_qkv.shape[2] // hd - n_heads) // 2
    r_heads = ret_a_fwd.shape[1]
    dk = d // r_heads
    dv = 2 * dk
    qd, vd = r_heads * dk, r_heads * dv
    assert hd == LANES and dk == 2 * LANES
    ml, mc = batch * s_len, batch * c_len

    n_rows = -(-(batch + 1) // 8) * 8
    cvec = jnp.zeros((n_rows, d), F32).at[:batch].set(c).at[batch].set(c_ctx)
    mods = _modulation(cvec, mod_w, mod_b)

    def mod_tables(i):
        sh1, sc1, g1, sh2, sc2, g2 = [a[:batch + 1, None, :] for a in jnp.split(mods[i], 6, axis=-1)]
        wm1 = norm1_w[i][None, None, :] * (1.0 + sc1)
        wm2 = norm2_w[i][None, None, :] * (1.0 + sc2)
        return wm1, sh1, g1, wm2, sh2, g2

    x2 = x.reshape(ml, d)
    ctx2 = ctx.reshape(mc, d)

    wm1, sh1, g1, wm2, sh2, g2 = mod_tables(0)
    w_qkv = attn_w_qkv[0].astype(BF16)
    nq = n_heads * hd
    qkv_lat = _norm_matmul(x2, wm1[:batch], sh1[:batch], w_qkv, group_rows=s_len, out_dtype=BF16,
                           rot="axial", tabs=_axial_tables(s_len, hd), rot_cols=nq + n_kv * hd,
                           name="qkv_latent")
    qkv_ctx = _norm_matmul(ctx2, wm1[batch:], sh1[batch:], w_qkv, group_rows=mc, out_dtype=BF16,
                           name="qkv_context")
    o_lat, o_ctx = _attention(qkv_lat, qkv_ctx, attn_sinks[0], batch=batch, s_len=s_len, c_len=c_len,
                              n_heads=n_heads, n_kv=n_kv, hd=hd)
    w_o = attn_w_o[0].astype(BF16)
    h_lat = _matmul_residual(o_lat, w_o, x2, g1[:batch], group_rows=s_len, name="attn_out_latent")
    h_ctx = _matmul_residual(o_ctx, w_o, ctx2, g1[batch:], group_rows=mc, name="attn_out_context")
    w1_all = exp_w1.astype(BF16)
    w2_all = exp_w2.astype(BF16)
    h_lat, h_ctx = _moe(h_lat, h_ctx, wm2, sh2, g2, router_w[0], router_b[0], w1_all, exp_b1, w2_all, exp_b2,
                        layer=0, s_len=s_len)

    wm1, sh1, g1, wm2, sh2, g2 = mod_tables(1)
    w_in = ret_w_in[0].astype(BF16)
    k_scale = dk ** -0.5
    p_lat = _norm_matmul(h_lat, wm1[:batch], sh1[:batch], w_in, group_rows=s_len, out_dtype=F32,
                         rot="half", tabs=_retention_tables(s_len, dk), rot_cols=2 * qd,
                         k_lo=qd, k_hi=2 * qd, k_scale=k_scale, name="ret_in_latent")
    p_ctx = _norm_matmul(h_ctx, wm1[batch:], sh1[batch:], w_in, group_rows=mc, out_dtype=F32,
                         col_off=qd, n_out=qd + vd, k_lo=qd, k_hi=2 * qd, k_scale=k_scale,
                         name="ret_in_context")
    lg = jnp.stack([jnp.log1p(-jnp.exp(ret_a_fwd[0].astype(F32))), jnp.log1p(-jnp.exp(ret_a_bwd[0].astype(F32)))])
    r_lat = _retention(p_lat, p_ctx, lg, batch=batch, s_len=s_len, c_len=c_len, n_heads=r_heads, dk=dk, dv=dv)
    h_lat = _matmul_residual(r_lat, ret_w_out[0].astype(BF16), h_lat, g1[:batch], group_rows=s_len,
                             name="ret_out_latent")
    out, _ = _moe(h_lat, None, wm2, sh2, g2, router_w[1], router_b[1], w1_all, exp_b1, w2_all, exp_b2,
                  layer=1, s_len=s_len, final_w=final_norm_w)
    return out.reshape(batch, s_len, d)
```

```python
import functools
import math

import jax
import jax.numpy as jnp
from jax import lax
from jax.experimental import pallas as pl
from jax.experimental.pallas import tpu as pltpu

F32 = jnp.float32
BF16 = jnp.bfloat16
I32 = jnp.int32

GRID_W = 64
ATTN_WINDOW = 128
ATTN_BLOCK = 128
ATTN_SOFTMAX_ROWS = 32
RET_CHUNK = 128
ROPE_BASE = 10000.0
TOP_K = 4
SWIGLU_LIMIT = 7.0
SWIGLU_ALPHA = 1.702
NORM_EPS = 1e-6

LANES = 128
NORM_ROWS = 16
NORM_UNROLL = 4
DMA_UNROLL = 32
GATHER_BUFS = 3
MM_WEIGHT_ELEMS = 4 * 1024 * 1024
MM_OUT_BLOCK_BYTES = 2 * 1024 * 1024
VMEM_LIMIT = 58 * 1024 * 1024

_NT = (((1,), (1,)), ((), ()))
_TN = (((0,), (0,)), ((), ()))


def _params(*sem):
    return pltpu.CompilerParams(dimension_semantics=sem, vmem_limit_bytes=VMEM_LIMIT)


def _tile(dim, pref, *also):
    t = min(dim, pref)
    while any(e % t for e in (dim,) + also):
        t //= 2
    assert t >= 8, (dim, pref, also)
    return t


def _mod_kernel(s_ref, w_ref, b_ref, o_ref):
    s = s_ref[...]
    s = s * jax.nn.sigmoid(s)
    o_ref[...] = jnp.dot(s.astype(BF16), w_ref[...].astype(BF16), preferred_element_type=F32) + b_ref[...]


def _modulation(cvec, mod_w, mod_b):
    nl, d, n = mod_w.shape
    r = cvec.shape[0]
    tn = _tile(n, 512)
    return pl.pallas_call(
        _mod_kernel,
        grid=(nl, n // tn),
        in_specs=[
            pl.BlockSpec((r, d), lambda l, j: (0, 0)),
            pl.BlockSpec((None, d, tn), lambda l, j: (l, 0, j)),
            pl.BlockSpec((None, 1, tn), lambda l, j: (l, 0, j)),
        ],
        out_specs=pl.BlockSpec((None, r, tn), lambda l, j: (l, 0, j)),
        out_shape=jax.ShapeDtypeStruct((nl, r, n), F32),
        compiler_params=_params("arbitrary", "arbitrary"),
        name="modulation",
    )(cvec, mod_w, mod_b.reshape(nl, 1, n))


def _norm_rows(x, wm, sh, d):
    ms = jnp.sum(x * x, axis=-1, keepdims=True) * (1.0 / d)
    return x * lax.rsqrt(ms + NORM_EPS) * wm + sh


def _nmm_kernel(*refs, d, tm, tn, col_off, rot, rot_cols, k_lo, k_hi, k_scale):
    if rot is None:
        x_ref, wm_ref, sh_ref, w_ref, o_ref, n_scr = refs
        cos_ref = sin_ref = None
    else:
        x_ref, wm_ref, sh_ref, w_ref, cos_ref, sin_ref, o_ref, n_scr = refs
    j = pl.program_id(1)

    @pl.when(j == 0)
    def _():
        wm = wm_ref[...]
        sh = sh_ref[...]

        def body(r, c):
            rows = pl.ds(pl.multiple_of(r * NORM_ROWS, NORM_ROWS), NORM_ROWS)
            n_scr[rows, :] = _norm_rows(x_ref[rows, :], wm, sh, d).astype(BF16)
            return c

        lax.fori_loop(0, tm // NORM_ROWS, body, 0, unroll=NORM_UNROLL)

    acc = jnp.dot(n_scr[...], w_ref[...], preferred_element_type=F32)
    col0 = j * tn + col_off
    if k_scale is not None:
        acc = acc * jnp.where((col0 >= k_lo) & (col0 < k_hi), k_scale, 1.0).astype(F32)

    if rot is None:
        o_ref[...] = acc.astype(o_ref.dtype)
        return

    cos = cos_ref[...]
    sin = sin_ref[...]
    if rot == "axial":
        lane = lax.broadcasted_iota(I32, (tm, LANES), 1)
        first = (lane & 32) == 0
        for hh in range(tn // LANES):
            a = acc[:, hh * LANES:(hh + 1) * LANES]
            partner = jnp.where(first, pltpu.roll(a, LANES - 32, 1), pltpu.roll(a, 32, 1))
            o_ref[:, hh * LANES:(hh + 1) * LANES] = (a * cos + partner * sin).astype(o_ref.dtype)
    else:
        for hh in range(tn // (2 * LANES)):
            c0 = hh * 2 * LANES
            x1 = acc[:, c0:c0 + LANES]
            x2 = acc[:, c0 + LANES:c0 + 2 * LANES]
            o_ref[:, c0:c0 + LANES] = (x1 * cos - x2 * sin).astype(o_ref.dtype)
            o_ref[:, c0 + LANES:c0 + 2 * LANES] = (x1 * sin + x2 * cos).astype(o_ref.dtype)


def _norm_matmul(h, wm, sh, w, *, group_rows, out_dtype, col_off=0, n_out=None, rot=None, tabs=None,
                 rot_cols=0, k_lo=0, k_hi=0, k_scale=None, name="norm_matmul"):
    m, d = h.shape
    n_out = w.shape[1] - col_off if n_out is None else n_out
    tm = _tile(m, 1024, group_rows)
    tn = min(MM_WEIGHT_ELEMS // d, MM_OUT_BLOCK_BYTES // (tm * jnp.dtype(out_dtype).itemsize))
    tn = _tile(n_out, tn, col_off, rot_cols, k_lo, k_hi)
    in_specs = [
        pl.BlockSpec((tm, d), lambda i, j: (i, 0), pipeline_mode=pl.Buffered(1)),
        pl.BlockSpec((None, 1, d), lambda i, j: ((i * tm) // group_rows, 0, 0)),
        pl.BlockSpec((None, 1, d), lambda i, j: ((i * tm) // group_rows, 0, 0)),
        pl.BlockSpec((d, tn), lambda i, j: (0, j + col_off // tn)),
    ]
    args = [h, wm, sh, w]
    if rot is not None:
        cos, sin = tabs
        s_pos = cos.shape[0]
        assert s_pos % tm == 0 and rot_cols % tn == 0
        nper = s_pos // tm
        cos2 = jnp.stack([cos, jnp.ones_like(cos)])
        sin2 = jnp.stack([sin, jnp.zeros_like(sin)])
        in_specs += [pl.BlockSpec((None, tm, LANES),
                                  lambda i, j: (jnp.where(j * tn + col_off < rot_cols, 0, 1), i % nper, 0))] * 2
        args += [cos2, sin2]
    kern = functools.partial(_nmm_kernel, d=d, tm=tm, tn=tn, col_off=col_off, rot=rot, rot_cols=rot_cols,
                             k_lo=k_lo, k_hi=k_hi, k_scale=k_scale)
    return pl.pallas_call(
        kern,
        grid=(m // tm, n_out // tn),
        in_specs=in_specs,
        out_specs=pl.BlockSpec((tm, tn), lambda i, j: (i, j)),
        out_shape=jax.ShapeDtypeStruct((m, n_out), out_dtype),
        scratch_shapes=[pltpu.VMEM((tm, d), BF16)],
        compiler_params=_params("arbitrary", "arbitrary"),
        name=name,
    )(*args)


def _mmres_kernel(a_ref, w_ref, h_ref, g_ref, o_ref):
    acc = jnp.dot(a_ref[...], w_ref[...], preferred_element_type=F32)
    o_ref[...] = h_ref[...] + g_ref[...] * acc


def _matmul_residual(a, w, h, gate, *, group_rows, name="matmul_residual"):
    m, k = a.shape
    n = w.shape[1]
    tm = _tile(m, 1024, group_rows)
    tn = _tile(n, max(LANES, MM_WEIGHT_ELEMS // k))
    return pl.pallas_call(
        _mmres_kernel,
        grid=(m // tm, n // tn),
        in_specs=[
            pl.BlockSpec((tm, k), lambda i, j: (i, 0), pipeline_mode=pl.Buffered(1)),
            pl.BlockSpec((k, tn), lambda i, j: (0, j)),
            pl.BlockSpec((tm, tn), lambda i, j: (i, j)),
            pl.BlockSpec((None, 1, tn), lambda i, j: ((i * tm) // group_rows, 0, j)),
        ],
        out_specs=pl.BlockSpec((tm, tn), lambda i, j: (i, j)),
        out_shape=jax.ShapeDtypeStruct((m, n), F32),
        compiler_params=_params("arbitrary", "arbitrary"),
        name=name,
    )(a, w, h, gate)


def _softmax_pv(s_parts, v_parts, sink_col):
    m = sink_col
    for s in s_parts:
        m = jnp.maximum(m, jnp.max(s, axis=-1, keepdims=True))
    es = [jnp.exp(s - m) for s in s_parts]
    den = jnp.exp(sink_col - m)
    for e in es:
        den = den + jnp.sum(e, axis=-1, keepdims=True)
    inv = 1.0 / den
    o = None
    for e, v in zip(es, v_parts):
        t = jnp.dot((e * inv).astype(BF16), v, preferred_element_type=F32)
        o = t if o is None else o + t
    return o


def _sink_column(sink_ref, kv, g_heads, rows):
    return jnp.concatenate([jnp.full((rows, 1), sink_ref[kv * g_heads + g], F32) for g in range(g_heads)], axis=0)


def _attn_lat_kernel(sink_ref, q_ref, k_ref, v_ref, kc_ref, vc_ref, o_ref, s_scr, p_scr, bias_scr, *,
                     s_len, c_len, tq, g_heads, hd, scale):
    kv = pl.program_id(1)
    qi = pl.program_id(2)
    blk = ATTN_BLOCK
    win = 3 * blk
    nblk = tq // blk
    rc = ATTN_SOFTMAX_ROWS
    kc = kc_ref[...]
    vc = vc_ref[...]
    rows = lax.broadcasted_iota(I32, (blk, win), 0)
    cols = lax.broadcasted_iota(I32, (blk, win), 1)
    rel = rows - cols + ATTN_WINDOW

    def body(bi, c):
        n = qi * nblk + bi
        start = pl.multiple_of(jnp.clip((n - 1) * blk, 0, s_len - win), blk)
        dist = rel + (n * blk - start)
        bias_scr[...] = jnp.where((dist >= 0) & (dist <= 2 * ATTN_WINDOW), 0.0, -jnp.inf)
        r0 = pl.multiple_of(bi * blk, blk)
        q4 = jnp.concatenate([q_ref[pl.ds(r0, blk), g * hd:(g + 1) * hd] for g in range(g_heads)], axis=0)
        s_scr[:, 0:win] = lax.dot_general(q4, k_ref[pl.ds(start, win), :], _NT, preferred_element_type=F32)
        s_scr[:, win:win + c_len] = lax.dot_general(q4, kc, _NT, preferred_element_type=F32)

        def softmax_rows(ci):
            rr = pl.ds(ci * rc, rc)
            qrow = pl.ds((ci * rc) % blk, rc)
            sink = sink_ref[kv * g_heads + (ci * rc) // blk]
            s_loc = s_scr[rr, 0:win] * scale + bias_scr[qrow, :]
            s_ctx = s_scr[rr, win:win + c_len] * scale
            m = jnp.maximum(jnp.maximum(jnp.max(s_loc, axis=-1, keepdims=True),
                                        jnp.max(s_ctx, axis=-1, keepdims=True)), sink)
            e_loc = jnp.exp(s_loc - m)
            e_ctx = jnp.exp(s_ctx - m)
            den = (jnp.sum(e_loc, axis=-1, keepdims=True) + jnp.sum(e_ctx, axis=-1, keepdims=True)
                   + jnp.exp(sink - m))
            inv = 1.0 / den
            p_scr[rr, 0:win] = (e_loc * inv).astype(p_scr.dtype)
            p_scr[rr, win:win + c_len] = (e_ctx * inv).astype(p_scr.dtype)

        for ci in range((g_heads * blk) // rc):
            softmax_rows(ci)
        o = (jnp.dot(p_scr[:, 0:win], v_ref[pl.ds(start, win), :], preferred_element_type=F32)
             + jnp.dot(p_scr[:, win:win + c_len], vc, preferred_element_type=F32))
        for g in range(g_heads):
            o_ref[pl.ds(r0, blk), g * hd:(g + 1) * hd] = o[g * blk:(g + 1) * blk].astype(o_ref.dtype)
        return c

    lax.fori_loop(0, nblk, body, 0)


def _attn_ctx_kernel(sink_ref, q_ref, kc_ref, vc_ref, o_ref, *, c_len, g_heads, hd, scale):
    kv = pl.program_id(1)
    sink_col = _sink_column(sink_ref, kv, g_heads, c_len)
    q4 = jnp.concatenate([q_ref[:, g * hd:(g + 1) * hd] for g in range(g_heads)], axis=0)
    kc = kc_ref[...]
    s = lax.dot_general(q4, kc, _NT, preferred_element_type=F32) * scale
    o = _softmax_pv([s], [vc_ref[...]], sink_col)
    for g in range(g_heads):
        o_ref[:, g * hd:(g + 1) * hd] = o[g * c_len:(g + 1) * c_len].astype(o_ref.dtype)


def _attention(qkv_lat, qkv_ctx, sinks, *, batch, s_len, c_len, n_heads, n_kv, hd):
    g_heads = n_heads // n_kv
    nq = n_heads * hd
    scale = hd ** -0.5
    tq = _tile(s_len, 1024)
    nqt = s_len // tq
    qw = g_heads * hd
    kcol = nq // hd
    vcol = kcol + n_kv
    smem = pl.BlockSpec(memory_space=pltpu.SMEM)
    lat = pl.pallas_call(
        functools.partial(_attn_lat_kernel, s_len=s_len, c_len=c_len, tq=tq, g_heads=g_heads, hd=hd, scale=scale),
        grid=(batch, n_kv, nqt),
        scratch_shapes=[
            pltpu.VMEM((g_heads * ATTN_BLOCK, 3 * ATTN_BLOCK + c_len), F32),
            pltpu.VMEM((g_heads * ATTN_BLOCK, 3 * ATTN_BLOCK + c_len), BF16),
            pltpu.VMEM((ATTN_BLOCK, 3 * ATTN_BLOCK), F32),
        ],
        in_specs=[
            smem,
            pl.BlockSpec((tq, qw), lambda b, h, i: (b * nqt + i, h)),
            pl.BlockSpec((s_len, hd), lambda b, h, i: (b, kcol + h)),
            pl.BlockSpec((s_len, hd), lambda b, h, i: (b, vcol + h)),
            pl.BlockSpec((c_len, hd), lambda b, h, i: (b, kcol + h)),
            pl.BlockSpec((c_len, hd), lambda b, h, i: (b, vcol + h)),
        ],
        out_specs=pl.BlockSpec((tq, qw), lambda b, h, i: (b * nqt + i, h)),
        out_shape=jax.ShapeDtypeStruct((batch * s_len, nq), BF16),
        compiler_params=_params("arbitrary", "arbitrary", "arbitrary"),
        name="attention_latent",
    )(sinks, qkv_lat, qkv_lat, qkv_lat, qkv_ctx, qkv_ctx)
    ctx = pl.pallas_call(
        functools.partial(_attn_ctx_kernel, c_len=c_len, g_heads=g_heads, hd=hd, scale=scale),
        grid=(batch, n_kv),
        in_specs=[
            smem,
            pl.BlockSpec((c_len, qw), lambda b, h: (b, h)),
            pl.BlockSpec((c_len, hd), lambda b, h: (b, kcol + h)),
            pl.BlockSpec((c_len, hd), lambda b, h: (b, vcol + h)),
        ],
        out_specs=pl.BlockSpec((c_len, qw), lambda b, h: (b, h)),
        out_shape=jax.ShapeDtypeStruct((batch * c_len, nq), BF16),
        compiler_params=_params("arbitrary", "arbitrary"),
        name="attention_context",
    )(sinks, qkv_ctx, qkv_ctx, qkv_ctx)
    return lat, ctx


def _ret_kernel(lg_ref, q_ref, k_ref, v_ref, g_ref, kc_ref, vc_ref, o_ref, s_scr, of_scr, *,
                n_seg, ts, c_len, dv):
    h = pl.program_id(1)
    j = pl.program_id(2)
    ch = RET_CHUNK
    nch = ts // ch

    def sweep(bwd):
        seg = 2 * n_seg - 1 - j if bwd else j
        lg = lg_ref[1 if bwd else 0, h]
        qi = lax.broadcasted_iota(I32, (ch, ch), 0)
        ki = lax.broadcasted_iota(I32, (ch, ch), 1)
        diff = ((ki - qi) if bwd else (qi - ki)).astype(F32)
        d_mask = jnp.exp(jnp.where(diff >= 0, diff * lg, -jnp.inf))
        pos = lax.broadcasted_iota(I32, (ch, 1), 0).astype(F32)
        fpos = ch - 1.0 - pos if bwd else pos
        xi = jnp.exp((fpos + 1.0) * lg)
        zeta = jnp.exp((ch - 1.0 - fpos) * lg)
        chunk_decay = jnp.exp(jnp.full((1, 1), ch, F32) * lg)

        @pl.when(j == (n_seg if bwd else 0))
        def _():
            cpos = lax.broadcasted_iota(I32, (c_len, 1), 0).astype(F32)
            w = jnp.exp((cpos if bwd else c_len - 1.0 - cpos) * lg)
            s_scr[...] = lax.dot_general((kc_ref[...] * w).astype(BF16), vc_ref[...].astype(BF16), _TN,
                                         preferred_element_type=F32)

        def chunk(ci, c):
            cc = nch - 1 - ci if bwd else ci
            r0 = pl.multiple_of(cc * ch, ch)
            rows = pl.ds(r0, ch)
            q = q_ref[rows, :].astype(BF16)
            kf = k_ref[rows, :]
            v = v_ref[rows, :].astype(BF16)
            s = s_scr[...]
            inner = lax.dot_general(q, kf.astype(BF16), _NT, preferred_element_type=F32) * d_mask
            o = jnp.dot(inner.astype(BF16), v, preferred_element_type=F32)
            o = o + jnp.dot(q, s.astype(BF16), preferred_element_type=F32) * xi
            s_scr[...] = chunk_decay * s + lax.dot_general((kf * zeta).astype(BF16), v, _TN,
                                                           preferred_element_type=F32)
            grow = pl.ds(pl.multiple_of(seg * ts + r0, ch), ch)
            if bwd:
                t = of_scr[grow, :] + o
                t = t * lax.rsqrt(jnp.sum(t * t, axis=-1, keepdims=True) * (1.0 / dv) + NORM_EPS)
                g = g_ref[rows, :]
                o_ref[rows, :] = (g * jax.nn.sigmoid(g) * t).astype(o_ref.dtype)
            else:
                of_scr[grow, :] = o
            return c

        lax.fori_loop(0, nch, chunk, 0, unroll=4)

    pl.when(j < n_seg)(lambda: sweep(False))
    pl.when(j >= n_seg)(lambda: sweep(True))


def _retention(p_lat, p_ctx, lg, *, batch, s_len, c_len, n_heads, dk, dv):
    qd = n_heads * dk
    vd = n_heads * dv
    ts = _tile(s_len, 1024)
    n_seg = s_len // ts

    def seg_of(j):
        return jnp.where(j >= n_seg, 2 * n_seg - 1 - j, j)

    def seg_bwd(j):
        return jnp.where(j >= n_seg, 2 * n_seg - 1 - j, n_seg - 1)

    kq = qd // dk
    kv_ = (2 * qd) // dv
    kg = (2 * qd + vd) // dv
    return pl.pallas_call(
        functools.partial(_ret_kernel, n_seg=n_seg, ts=ts, c_len=c_len, dv=dv),
        grid=(batch, n_heads, 2 * n_seg),
        in_specs=[
            pl.BlockSpec(memory_space=pltpu.SMEM),
            pl.BlockSpec((ts, dk), lambda b, h, j: (b * n_seg + seg_of(j), h)),
            pl.BlockSpec((ts, dk), lambda b, h, j: (b * n_seg + seg_of(j), kq + h)),
            pl.BlockSpec((ts, dv), lambda b, h, j: (b * n_seg + seg_of(j), kv_ + h)),
            pl.BlockSpec((ts, dv), lambda b, h, j: (b * n_seg + seg_bwd(j), kg + h)),
            pl.BlockSpec((c_len, dk), lambda b, h, j: (b, h)),
            pl.BlockSpec((c_len, dv), lambda b, h, j: (b, qd // dv + h)),
        ],
        out_specs=pl.BlockSpec((ts, dv), lambda b, h, j: (b * n_seg + seg_bwd(j), h)),
        out_shape=jax.ShapeDtypeStruct((batch * s_len, vd), BF16),
        scratch_shapes=[pltpu.VMEM((dk, dv), F32), pltpu.VMEM((s_len, dv), F32)],
        compiler_params=_params("arbitrary", "arbitrary", "arbitrary"),
        name="retention",
    )(lg, p_lat, p_lat, p_lat, p_lat, p_ctx, p_ctx)


def _router_kernel(*refs, n_src, nl_tiles, tm, d, n_exp):
    if n_src == 2:
        hl_ref, hc_ref, wm_ref, sh_ref, w_ref, b_ref, n_ref, idx_ref, w8_ref, r8_ref, cnt_ref, carry, nhi, nlo = refs
    else:
        hl_ref, wm_ref, sh_ref, w_ref, b_ref, n_ref, idx_ref, w8_ref, r8_ref, cnt_ref, carry, nhi, nlo = refs
        hc_ref = None
    i = pl.program_id(0)
    dh = d // 2

    @pl.when(i == 0)
    def _():
        carry[...] = jnp.zeros_like(carry)

    wm = wm_ref[...]
    sh = sh_ref[...]

    def fill(src_ref):
        def body(r, c):
            rows = pl.ds(pl.multiple_of(r * NORM_ROWS, NORM_ROWS), NORM_ROWS)
            n = _norm_rows(src_ref[rows, :], wm, sh, d)
            hi = n.astype(BF16)
            hi32 = hi.astype(F32)
            bits = lax.bitcast_convert_type(hi32, jnp.uint32)
            n_ref[rows, :] = (bits[:, :dh] >> 16) | (bits[:, dh:] & jnp.uint32(0xFFFF0000))
            nhi[rows, :] = hi
            nlo[rows, :] = (n - hi32).astype(BF16)
            return c

        lax.fori_loop(0, tm // NORM_ROWS, body, 0, unroll=NORM_UNROLL)

    if n_src == 2:
        pl.when(i < nl_tiles)(lambda: fill(hl_ref))
        pl.when(i >= nl_tiles)(lambda: fill(hc_ref))
    else:
        fill(hl_ref)

    w_hi = w_ref[0:n_exp, :]
    w_lo = w_ref[n_exp:2 * n_exp, :]
    x_hi = nhi[...]
    logits = (lax.dot_general(w_hi, x_hi, _NT, preferred_element_type=F32)
              + lax.dot_general(w_lo, x_hi, _NT, preferred_element_type=F32)
              + lax.dot_general(w_hi, nlo[...], _NT, preferred_element_type=F32)
              + b_ref[...])

    eidx = lax.broadcasted_iota(I32, (n_exp, tm), 0)
    row8 = lax.broadcasted_iota(I32, (8, tm), 0)
    rest = logits
    sel = jnp.zeros((n_exp, tm), jnp.bool_)
    picks = []
    idx8 = jnp.zeros((8, tm), I32)
    top1 = None
    for k in range(TOP_K):
        mx = jnp.max(rest, axis=0, keepdims=True)
        if k == 0:
            top1 = mx
        am = jnp.min(jnp.where(rest == mx, eidx, n_exp), axis=0, keepdims=True)
        pick = eidx == am
        picks.append(pick)
        sel = sel | pick
        idx8 = jnp.where(row8 == k, am, idx8)
        rest = jnp.where(pick, -jnp.inf, rest)
    e = jnp.where(sel, jnp.exp(logits - top1), 0.0)
    gates = e / jnp.sum(e, axis=0, keepdims=True)

    t0 = lax.broadcasted_iota(I32, (tm, tm), 0)
    t1 = lax.broadcasted_iota(I32, (tm, tm), 1)
    before = jnp.where(t0 < t1, 1.0, 0.0).astype(BF16)
    self_f = jnp.where(sel, 1.0, 0.0)
    rank = jnp.dot(self_f.astype(BF16), before, preferred_element_type=F32) + carry[:, 0:1]
    w8 = jnp.zeros((8, tm), F32)
    r8 = jnp.zeros((8, tm), F32)
    for k, pick in enumerate(picks):
        w8 = jnp.where(row8 == k, jnp.sum(jnp.where(pick, gates, 0.0), axis=0, keepdims=True), w8)
        r8 = jnp.where(row8 == k, jnp.sum(jnp.where(pick, rank, 0.0), axis=0, keepdims=True), r8)
    idx_ref[...] = idx8
    w8_ref[...] = w8
    r8_ref[...] = r8.astype(I32)
    new_carry = carry[...] + jnp.sum(self_f, axis=1, keepdims=True)
    carry[...] = new_carry
    cnt_ref[...] = new_carry


def _router(h_lat, h_ctx, wm, sh, rw, rb, *, group_rows, tm):
    ml, d = h_lat.shape
    n_exp = rw.shape[1]
    mc = 0 if h_ctx is None else h_ctx.shape[0]
    t = ml + mc
    assert ml % tm == 0 and mc % tm == 0 and group_rows % tm == 0
    nl_tiles = ml // tm
    n_groups_lat = ml // group_rows
    rw_t = rw.T
    w_hi = rw_t.astype(BF16)
    w_lo = (rw_t - w_hi.astype(F32)).astype(BF16)
    w3 = jnp.concatenate([w_hi, w_lo], axis=0)

    def grp(i):
        return jnp.where(i < nl_tiles, (i * tm) // group_rows, n_groups_lat)

    in_specs = [pl.BlockSpec((tm, d), lambda i: (jnp.minimum(i, nl_tiles - 1), 0))]
    args = [h_lat]
    if mc:
        in_specs.append(pl.BlockSpec((tm, d), lambda i: (jnp.maximum(i - nl_tiles, 0), 0)))
        args.append(h_ctx)
    in_specs += [
        pl.BlockSpec((None, 1, d), lambda i: (grp(i), 0, 0)),
        pl.BlockSpec((None, 1, d), lambda i: (grp(i), 0, 0)),
        pl.BlockSpec((2 * n_exp, d), lambda i: (0, 0)),
        pl.BlockSpec((n_exp, 1), lambda i: (0, 0)),
    ]
    args += [wm, sh, w3, rb.reshape(n_exp, 1)]
    return pl.pallas_call(
        functools.partial(_router_kernel, n_src=2 if mc else 1, nl_tiles=nl_tiles, tm=tm, d=d, n_exp=n_exp),
        grid=(t // tm,),
        in_specs=in_specs,
        out_specs=[
            pl.BlockSpec((tm, d // 2), lambda i: (i, 0)),
            pl.BlockSpec((8, tm), lambda i: (0, i)),
            pl.BlockSpec((8, tm), lambda i: (0, i)),
            pl.BlockSpec((8, tm), lambda i: (0, i)),
            pl.BlockSpec((n_exp, LANES), lambda i: (0, 0)),
        ],
        out_shape=[
            jax.ShapeDtypeStruct((t, d // 2), jnp.uint32),
            jax.ShapeDtypeStruct((8, t), I32),
            jax.ShapeDtypeStruct((8, t), F32),
            jax.ShapeDtypeStruct((8, t), I32),
            jax.ShapeDtypeStruct((n_exp, LANES), F32),
        ],
        scratch_shapes=[pltpu.VMEM((n_exp, LANES), F32), pltpu.VMEM((tm, d), BF16), pltpu.VMEM((tm, d), BF16)],
        compiler_params=_params("arbitrary"),
        name="moe_router",
    )(*args)


def _routing_tables(idx8, r8, cnt, *, tm, n_tiles):
    n_exp = cnt.shape[0]
    c = cnt[:, 0].astype(I32)
    pc = ((c + tm - 1) // tm) * tm
    cum = jnp.cumsum(pc)
    off = cum - pc
    idx4 = idx8[:TOP_K]
    off4 = jnp.sum(jnp.where(idx4[:, :, None] == jnp.arange(n_exp, dtype=I32), off, 0), axis=-1)
    pos4 = (off4 + r8[:TOP_K]).T.reshape(-1)
    extents = jnp.stack([cum[-1] // tm, jnp.asarray(n_tiles * tm, I32), jnp.asarray(idx8.shape[1], I32)]).astype(I32)
    tile_start = jnp.arange(n_tiles, dtype=I32) * tm
    tile_expert = jnp.minimum(jnp.sum(tile_start[:, None] >= cum[None, :], axis=1), n_exp - 1).astype(I32)
    return tile_expert, extents, pos4


def _moe_kernel(te_ref, nu_ref, pos_ref, n_hbm, w1_ref, b1_ref, w2_ref, b2_ref, o_ref, tok_ref, xbuf, sem, *,
                tm, f_dim, dh):
    i = pl.program_id(0)
    n_used = nu_ref[0]
    slot = i % GATHER_BUFS

    def row_copy(tile, r, sl):
        tok = tok_ref[tile * tm + r]
        return pltpu.make_async_copy(n_hbm.at[pl.ds(tok, 1), :], xbuf.at[sl, pl.ds(r, 1), :], sem.at[sl])

    def wait_rows(sl):
        pltpu.make_async_copy(n_hbm.at[pl.ds(0, tm), :], xbuf.at[sl], sem.at[sl]).wait()

    def start_rows(tile, sl):
        def body(rb, c):
            for u in range(DMA_UNROLL):
                row_copy(tile, rb * DMA_UNROLL + u, sl).start()
            return c

        lax.fori_loop(0, tm // DMA_UNROLL, body, 0)

    @pl.when(i == 0)
    def _():
        def zero(jb, c):
            for u in range(DMA_UNROLL):
                tok_ref[jb * DMA_UNROLL + u] = 0
            return c

        lax.fori_loop(0, nu_ref[1] // DMA_UNROLL, zero, 0)

        def invert(t, c):
            for k in range(TOP_K):
                tok_ref[pos_ref[t * TOP_K + k]] = t
            return c

        lax.fori_loop(0, nu_ref[2], invert, 0)
        for b in range(GATHER_BUFS - 1):
            start_rows(jnp.minimum(b, n_used - 1), b)

    def step(sl):
        ahead = jnp.minimum(i + GATHER_BUFS - 1, n_used - 1)
        for r in range(tm):
            row_copy(ahead, r, (sl + GATHER_BUFS - 1) % GATHER_BUFS).start()
        wait_rows(sl)
        words = xbuf[sl]
        x_lo = lax.bitcast_convert_type(words << 16, F32).astype(BF16)
        x_hi = lax.bitcast_convert_type(words & jnp.uint32(0xFFFF0000), F32).astype(BF16)
        gu = (jnp.dot(x_lo, w1_ref[0:dh, :], preferred_element_type=F32)
              + jnp.dot(x_hi, w1_ref[dh:2 * dh, :], preferred_element_type=F32) + b1_ref[...])
        gate = jnp.minimum(gu[:, :f_dim], SWIGLU_LIMIT)
        up = jnp.clip(gu[:, f_dim:], -SWIGLU_LIMIT, SWIGLU_LIMIT)
        act = (up + 1.0) * (gate * jax.nn.sigmoid(SWIGLU_ALPHA * gate))
        o_ref[...] = jnp.dot(act.astype(BF16), w2_ref[...], preferred_element_type=F32) + b2_ref[...]

        @pl.when(i == n_used - 1)
        def _():
            for ds in range(1, GATHER_BUFS):
                wait_rows((sl + ds) % GATHER_BUFS)

    for s in range(GATHER_BUFS):
        pl.when((i < n_used) & (slot == s))(functools.partial(step, s))

    @pl.when(i >= n_used)
    def _():
        o_ref[...] = jnp.zeros_like(o_ref)


def _experts(n_packed, tile_expert, n_used, pos4, w1, b1, w2, b2, *, layer, tm, n_tiles):
    n_layers, n_exp, d, f2 = w1.shape
    f_dim = f2 // 2
    dh = n_packed.shape[1]
    grid_spec = pltpu.PrefetchScalarGridSpec(
        num_scalar_prefetch=3,
        grid=(n_tiles,),
        in_specs=[
            pl.BlockSpec(memory_space=pl.ANY),
            pl.BlockSpec((None, None, d, f2), lambda i, te, nu, pos: (layer, te[i], 0, 0)),
            pl.BlockSpec((None, None, 1, f2), lambda i, te, nu, pos: (layer, te[i], 0, 0)),
            pl.BlockSpec((None, None, f_dim, d), lambda i, te, nu, pos: (layer, te[i], 0, 0)),
            pl.BlockSpec((None, None, 1, d), lambda i, te, nu, pos: (layer, te[i], 0, 0)),
        ],
        out_specs=pl.BlockSpec((tm, d), lambda i, te, nu, pos: (i, 0)),
        scratch_shapes=[pltpu.SMEM((n_tiles * tm,), I32), pltpu.VMEM((GATHER_BUFS, tm, dh), jnp.uint32),
                        pltpu.SemaphoreType.DMA((GATHER_BUFS,))],
    )
    return pl.pallas_call(
        functools.partial(_moe_kernel, tm=tm, f_dim=f_dim, dh=dh),
        grid_spec=grid_spec,
        out_shape=jax.ShapeDtypeStruct((n_tiles * tm, d), F32),
        compiler_params=_params("arbitrary"),
        name="moe_experts",
    )(tile_expert, n_used, pos4, n_packed, w1, b1.reshape(n_layers, n_exp, 1, f2), w2,
      b2.reshape(n_layers, n_exp, 1, d))


def _combine_kernel(*refs, tc, d, tok_base, final):
    if final:
        pos_ref, ys_hbm, h_ref, g_ref, w_ref, fw_ref, o_ref, buf, sem = refs
    else:
        pos_ref, ys_hbm, h_ref, g_ref, w_ref, o_ref, buf, sem = refs
        fw_ref = None
    i = pl.program_id(0)
    nt = pl.num_programs(0)
    slot = i % 2

    def start_rows(tile, r0, sl):
        for u in range(NORM_ROWS):
            base = (tok_base + tile * tc + r0 + u) * TOP_K
            for k in range(TOP_K):
                pltpu.make_async_copy(ys_hbm.at[pl.ds(pos_ref[base + k], 1), :],
                                      buf.at[sl, pl.ds(k * tc + r0 + u, 1), :], sem.at[sl]).start()

    def wait_rows(sl):
        pltpu.make_async_copy(ys_hbm.at[pl.ds(0, TOP_K * tc), :], buf.at[sl], sem.at[sl]).wait()

    @pl.when(i == 0)
    def _():
        def first(r, c):
            start_rows(0, r * NORM_ROWS, 0)
            return c

        lax.fori_loop(0, tc // NORM_ROWS, first, 0)

    gate = g_ref[...]
    fw = None if fw_ref is None else fw_ref[...]
    nxt = jnp.minimum(i + 1, nt - 1)

    def step(sl):
        wait_rows(sl)

        def body(r, c):
            r0 = pl.multiple_of(r * NORM_ROWS, NORM_ROWS)
            start_rows(nxt, r0, 1 - sl)
            wk = w_ref[pl.ds(r0, NORM_ROWS), :]
            y = wk[:, 0:1] * buf[sl, pl.ds(r0, NORM_ROWS), :]
            for k in range(1, TOP_K):
                y = y + wk[:, k:k + 1] * buf[sl, pl.ds(k * tc + r0, NORM_ROWS), :]
            out = h_ref[pl.ds(r0, NORM_ROWS), :] + gate * y
            if final:
                ms = jnp.sum(out * out, axis=-1, keepdims=True) * (1.0 / d)
                out = out * lax.rsqrt(ms + NORM_EPS) * fw
            o_ref[pl.ds(r0, NORM_ROWS), :] = out
            return c

        lax.fori_loop(0, tc // NORM_ROWS, body, 0)

        @pl.when(i == nt - 1)
        def _():
            wait_rows(1 - sl)

    pl.when(slot == 0)(lambda: step(0))
    pl.when(slot == 1)(lambda: step(1))


def _combine(ys, pos4, w4, h, gate, *, group_rows, tok_base, final_w=None):
    m, d = h.shape
    tc = _tile(m, 128)
    assert group_rows % tc == 0 and tok_base % tc == 0
    final = final_w is not None
    in_specs = [
        pl.BlockSpec(memory_space=pl.ANY),
        pl.BlockSpec((tc, d), lambda i, pos: (i, 0)),
        pl.BlockSpec((None, 1, d), lambda i, pos: ((i * tc) // group_rows, 0, 0)),
        pl.BlockSpec((tc, TOP_K), lambda i, pos: (tok_base // tc + i, 0)),
    ]
    args = [pos4, ys, h, gate, w4]
    if final:
        in_specs.append(pl.BlockSpec((1, d), lambda i, pos: (0, 0)))
        args.append(final_w.reshape(1, d))
    grid_spec = pltpu.PrefetchScalarGridSpec(
        num_scalar_prefetch=1,
        grid=(m // tc,),
        in_specs=in_specs,
        out_specs=pl.BlockSpec((tc, d), lambda i, pos: (i, 0)),
        scratch_shapes=[pltpu.VMEM((2, TOP_K * tc, d), F32), pltpu.SemaphoreType.DMA((2,))],
    )
    return pl.pallas_call(
        functools.partial(_combine_kernel, tc=tc, d=d, tok_base=tok_base, final=final),
        grid_spec=grid_spec,
        out_shape=jax.ShapeDtypeStruct((m, d), F32),
        compiler_params=_params("arbitrary"),
        name="moe_combine",
    )(*args)


def _moe(h_lat, h_ctx, wm, sh, gate, rw, rb, w1, b1, w2, b2, *, layer, s_len, final_w=None):
    ml = h_lat.shape[0]
    mc = 0 if h_ctx is None else h_ctx.shape[0]
    t = ml + mc
    n_exp = rw.shape[1]
    nb = ml // s_len
    tm_r = _tile(math.gcd(ml, mc) if mc else ml, 256)
    n_packed, idx8, w8, r8, cnt = _router(h_lat, h_ctx, wm, sh, rw, rb, group_rows=s_len, tm=tm_r)
    tm = 256
    n_tiles = (TOP_K * t) // tm + n_exp
    tile_expert, n_used, pos4 = _routing_tables(idx8, r8, cnt, tm=tm, n_tiles=n_tiles)
    w4 = w8[:TOP_K].T
    ys = _experts(n_packed, tile_expert, n_used, pos4, w1, b1, w2, b2, layer=layer, tm=tm, n_tiles=n_tiles)
    new_lat = _combine(ys, pos4, w4, h_lat, gate[:nb], group_rows=s_len, tok_base=0, final_w=final_w)
    new_ctx = None
    if mc:
        new_ctx = _combine(ys, pos4, w4, h_ctx, gate[nb:nb + 1], group_rows=mc, tok_base=ml)
    return new_lat, new_ctx


def _axial_tables(s_len, hd):
    half = hd // 2
    inv = 1.0 / (ROPE_BASE ** (jnp.arange(0, half, 2, dtype=F32) / half))
    t = jnp.arange(s_len)
    rows = (t // GRID_W).astype(F32)[:, None] * inv[None, :]
    cols = (t % GRID_W).astype(F32)[:, None] * inv[None, :]
    cos = jnp.concatenate([jnp.cos(rows), jnp.cos(rows), jnp.cos(cols), jnp.cos(cols)], axis=-1)
    sin = jnp.concatenate([-jnp.sin(rows), jnp.sin(rows), -jnp.sin(cols), jnp.sin(cols)], axis=-1)
    return cos, sin


def _retention_tables(s_len, dk):
    theta = 1.0 / (ROPE_BASE ** jnp.linspace(0.0, 1.0, dk // 2, dtype=F32))
    ang = jnp.arange(s_len, dtype=F32)[:, None] * theta[None, :]
    return jnp.cos(ang), jnp.sin(ang)


def kernel(x, c, ctx, c_ctx, mod_w, mod_b, norm1_w, norm2_w, attn_w_qkv, attn_w_o, attn_sinks, ret_w_in,
           ret_w_out, ret_a_fwd, ret_a_bwd, router_w, router_b, exp_w1, exp_b1, exp_w2, exp_b2, final_norm_w):
    batch, s_len, d = x.shape
    c_len = ctx.shape[1]
    depth = mod_w.shape[0]
    assert depth == 2 and attn_w_qkv.shape[0] == 1 and ret_w_in.shape[0] == 1
    n_heads = attn_sinks.shape[1]
    hd = d // n_heads
    n_kv = (attn_w_qkv.shape[2] // hd - n_heads) // 2
    r_heads = ret_a_fwd.shape[1]
    dk = d // r_heads
    dv = 2 * dk
    qd, vd = r_heads * dk, r_heads * dv
    assert hd == LANES and dk == 2 * LANES
    ml, mc = batch * s_len, batch * c_len

    n_rows = -(-(batch + 1) // 8) * 8
    cvec = jnp.zeros((n_rows, d), F32).at[:batch].set(c).at[batch].set(c_ctx)
    mods = _modulation(cvec, mod_w, mod_b)

    def mod_tables(i):
        sh1, sc1, g1, sh2, sc2, g2 = [a[:batch + 1, None, :] for a in jnp.split(mods[i], 6, axis=-1)]
        wm1 = norm1_w[i][None, None, :] * (1.0 + sc1)
        wm2 = norm2_w[i][None, None, :] * (1.0 + sc2)
        return wm1, sh1, g1, wm2, sh2, g2

    x2 = x.reshape(ml, d)
    ctx2 = ctx.reshape(mc, d)

    wm1, sh1, g1, wm2, sh2, g2 = mod_tables(0)
    w_qkv = attn_w_qkv[0].astype(BF16)
    nq = n_heads * hd
    qkv_lat = _norm_matmul(x2, wm1[:batch], sh1[:batch], w_qkv, group_rows=s_len, out_dtype=BF16,
                           rot="axial", tabs=_axial_tables(s_len, hd), rot_cols=nq + n_kv * hd,
                           name="qkv_latent")
    qkv_ctx = _norm_matmul(ctx2, wm1[batch:], sh1[batch:], w_qkv, group_rows=mc, out_dtype=BF16,
                           name="qkv_context")
    o_lat, o_ctx = _attention(qkv_lat, qkv_ctx, attn_sinks[0], batch=batch, s_len=s_len, c_len=c_len,
                              n_heads=n_heads, n_kv=n_kv, hd=hd)
    w_o = attn_w_o[0].astype(BF16)
    h_lat = _matmul_residual(o_lat, w_o, x2, g1[:batch], group_rows=s_len, name="attn_out_latent")
    h_ctx = _matmul_residual(o_ctx, w_o, ctx2, g1[batch:], group_rows=mc, name="attn_out_context")
    w1_all = exp_w1.astype(BF16)
    w2_all = exp_w2.astype(BF16)
    h_lat, h_ctx = _moe(h_lat, h_ctx, wm2, sh2, g2, router_w[0], router_b[0], w1_all, exp_b1, w2_all, exp_b2,
                        layer=0, s_len=s_len)

    wm1, sh1, g1, wm2, sh2, g2 = mod_tables(1)
    w_in = ret_w_in[0].astype(BF16)
    k_scale = dk ** -0.5
    p_lat = _norm_matmul(h_lat, wm1[:batch], sh1[:batch], w_in, group_rows=s_len, out_dtype=F32,
                         rot="half", tabs=_retention_tables(s_len, dk), rot_cols=2 * qd,
                         k_lo=qd, k_hi=2 * qd, k_scale=k_scale, name="ret_in_latent")
    p_ctx = _norm_matmul(h_ctx, wm1[batch:], sh1[batch:], w_in, group_rows=mc, out_dtype=F32,
                         col_off=qd, n_out=qd + vd, k_lo=qd, k_hi=2 * qd, k_scale=k_scale,
                         name="ret_in_context")
    lg = jnp.stack([jnp.log1p(-jnp.exp(ret_a_fwd[0].astype(F32))), jnp.log1p(-jnp.exp(ret_a_bwd[0].astype(F32)))])
    r_lat = _retention(p_lat, p_ctx, lg, batch=batch, s_len=s_len, c_len=c_len, n_heads=r_heads, dk=dk, dv=dv)
    h_lat = _matmul_residual(r_lat, ret_w_out[0].astype(BF16), h_lat, g1[:batch], group_rows=s_len,
                             name="ret_out_latent")
    out, _ = _moe(h_lat, None, wm2, sh2, g2, router_w[1], router_b[1], w1_all, exp_b1, w2_all, exp_b2,
                  layer=1, s_len=s_len, final_w=final_norm_w)
    return out.reshape(batch, s_len, d)
```

```python
import functools
import math

import jax
import jax.numpy as jnp
from jax import lax
from jax.experimental import pallas as pl
from jax.experimental.pallas import tpu as pltpu

F32 = jnp.float32
BF16 = jnp.bfloat16
I32 = jnp.int32

GRID_W = 64
ATTN_WINDOW = 128
ATTN_BLOCK = 128
ATTN_SOFTMAX_ROWS = 32
RET_CHUNK = 128
ROPE_BASE = 10000.0
TOP_K = 4
SWIGLU_LIMIT = 7.0
SWIGLU_ALPHA = 1.702
NORM_EPS = 1e-6

LANES = 128
NORM_ROWS = 16
NORM_UNROLL = 4
DMA_UNROLL = 32
GATHER_BUFS = 3
MM_WEIGHT_ELEMS = 4 * 1024 * 1024
MM_OUT_BLOCK_BYTES = 4 * 1024 * 1024
NORM_STREAM_ROWS = 128
VMEM_LIMIT = 58 * 1024 * 1024

_NT = (((1,), (1,)), ((), ()))
_TN = (((0,), (0,)), ((), ()))


def _params(*sem):
    return pltpu.CompilerParams(dimension_semantics=sem, vmem_limit_bytes=VMEM_LIMIT)


def _tile(dim, pref, *also):
    for t in range(min(dim, pref), 7, -1):
        if t % 8 == 0 and not any(e % t for e in (dim,) + also):
            return t
    raise ValueError((dim, pref, also))


def _mod_kernel(s_ref, w_ref, b_ref, o_ref):
    s = s_ref[...]
    s = s * jax.nn.sigmoid(s)
    o_ref[...] = jnp.dot(s.astype(BF16), w_ref[...].astype(BF16), preferred_element_type=F32) + b_ref[...]


def _modulation(cvec, mod_w, mod_b):
    nl, d, n = mod_w.shape
    r = cvec.shape[0]
    tn = _tile(n, 512)
    return pl.pallas_call(
        _mod_kernel,
        grid=(nl, n // tn),
        in_specs=[
            pl.BlockSpec((r, d), lambda l, j: (0, 0)),
            pl.BlockSpec((None, d, tn), lambda l, j: (l, 0, j)),
            pl.BlockSpec((None, 1, tn), lambda l, j: (l, 0, j)),
        ],
        out_specs=pl.BlockSpec((None, r, tn), lambda l, j: (l, 0, j)),
        out_shape=jax.ShapeDtypeStruct((nl, r, n), F32),
        compiler_params=_params("arbitrary", "arbitrary"),
        name="modulation",
    )(cvec, mod_w, mod_b.reshape(nl, 1, n))


def _norm_rows(x, wm, sh, d):
    ms = jnp.sum(x * x, axis=-1, keepdims=True) * (1.0 / d)
    return x * lax.rsqrt(ms + NORM_EPS) * wm + sh


def _nmm_kernel(*refs, d, tm, tn, col_off, rot, rot_cols, k_lo, k_hi, k_scale):
    if rot is None:
        x_hbm, wm_ref, sh_ref, w_ref, o_ref, n_scr, xbuf, xsem = refs
        cos_ref = sin_ref = None
    else:
        x_hbm, wm_ref, sh_ref, w_ref, cos_ref, sin_ref, o_ref, n_scr, xbuf, xsem = refs
    i = pl.program_id(0)
    j = pl.program_id(1)
    xr = xbuf.shape[1]
    n_chunks = tm // xr

    @pl.when(j == 0)
    def _():
        wm = wm_ref[...]
        sh = sh_ref[...]

        def x_copy(c, sl):
            return pltpu.make_async_copy(x_hbm.at[pl.ds(i * tm + c * xr, xr), :], xbuf.at[sl], xsem.at[sl])

        x_copy(0, 0).start()

        def chunk(c, carry):
            sl = c % 2

            @pl.when(c + 1 < n_chunks)
            def _():
                x_copy(c + 1, 1 - sl).start()

            x_copy(c, sl).wait()
            for u in range(xr // NORM_ROWS):
                n_scr[pl.ds(pl.multiple_of(c * xr + u * NORM_ROWS, NORM_ROWS), NORM_ROWS), :] = _norm_rows(
                    xbuf[sl, u * NORM_ROWS:(u + 1) * NORM_ROWS, :], wm, sh, d).astype(BF16)
            return carry

        lax.fori_loop(0, n_chunks, chunk, 0)

    acc = jnp.dot(n_scr[...], w_ref[...], preferred_element_type=F32)
    col0 = j * tn + col_off
    if k_scale is not None:
        acc = acc * jnp.where((col0 >= k_lo) & (col0 < k_hi), k_scale, 1.0).astype(F32)

    if rot is None:
        o_ref[...] = acc.astype(o_ref.dtype)
        return

    cos = cos_ref[...]
    sin = sin_ref[...]
    if rot == "axial":
        lane = lax.broadcasted_iota(I32, (tm, LANES), 1)
        first = (lane & 32) == 0
        for hh in range(tn // LANES):
            a = acc[:, hh * LANES:(hh + 1) * LANES]
            partner = jnp.where(first, pltpu.roll(a, LANES - 32, 1), pltpu.roll(a, 32, 1))
            o_ref[:, hh * LANES:(hh + 1) * LANES] = (a * cos + partner * sin).astype(o_ref.dtype)
    else:
        for hh in range(tn // (2 * LANES)):
            c0 = hh * 2 * LANES
            x1 = acc[:, c0:c0 + LANES]
            x2 = acc[:, c0 + LANES:c0 + 2 * LANES]
            o_ref[:, c0:c0 + LANES] = (x1 * cos - x2 * sin).astype(o_ref.dtype)
            o_ref[:, c0 + LANES:c0 + 2 * LANES] = (x1 * sin + x2 * cos).astype(o_ref.dtype)


def _norm_matmul(h, wm, sh, w, *, group_rows, out_dtype, col_off=0, n_out=None, rot=None, tabs=None,
                 rot_cols=0, k_lo=0, k_hi=0, k_scale=None, name="norm_matmul"):
    m, d = h.shape
    n_out = w.shape[1] - col_off if n_out is None else n_out
    tm = _tile(m, 1024, group_rows)
    tn = min(MM_WEIGHT_ELEMS // d, MM_OUT_BLOCK_BYTES // (tm * jnp.dtype(out_dtype).itemsize))
    tn = _tile(n_out, tn, col_off, rot_cols, k_lo, k_hi)
    xr = _tile(tm, NORM_STREAM_ROWS)
    in_specs = [
        pl.BlockSpec(memory_space=pl.ANY),
        pl.BlockSpec((None, 1, d), lambda i, j: ((i * tm) // group_rows, 0, 0)),
        pl.BlockSpec((None, 1, d), lambda i, j: ((i * tm) // group_rows, 0, 0)),
        pl.BlockSpec((d, tn), lambda i, j: (0, j + col_off // tn)),
    ]
    args = [h, wm, sh, w]
    if rot is not None:
        cos, sin = tabs
        s_pos = cos.shape[0]
        assert s_pos % tm == 0 and rot_cols % tn == 0
        nper = s_pos // tm
        cos2 = jnp.stack([cos, jnp.ones_like(cos)])
        sin2 = jnp.stack([sin, jnp.zeros_like(sin)])
        in_specs += [pl.BlockSpec((None, tm, LANES),
                                  lambda i, j: (jnp.where(j * tn + col_off < rot_cols, 0, 1), i % nper, 0))] * 2
        args += [cos2, sin2]
    kern = functools.partial(_nmm_kernel, d=d, tm=tm, tn=tn, col_off=col_off, rot=rot, rot_cols=rot_cols,
                             k_lo=k_lo, k_hi=k_hi, k_scale=k_scale)
    return pl.pallas_call(
        kern,
        grid=(m // tm, n_out // tn),
        in_specs=in_specs,
        out_specs=pl.BlockSpec((tm, tn), lambda i, j: (i, j)),
        out_shape=jax.ShapeDtypeStruct((m, n_out), out_dtype),
        scratch_shapes=[pltpu.VMEM((tm, d), BF16), pltpu.VMEM((2, xr, d), F32), pltpu.SemaphoreType.DMA((2,))],
        compiler_params=_params("arbitrary", "arbitrary"),
        name=name,
    )(*args)


def _mmres_kernel(a_ref, w_ref, h_ref, g_ref, o_ref):
    acc = jnp.dot(a_ref[...], w_ref[...], preferred_element_type=F32)
    o_ref[...] = h_ref[...] + g_ref[...] * acc


def _matmul_residual(a, w, h, gate, *, group_rows, name="matmul_residual"):
    m, k = a.shape
    n = w.shape[1]
    tm = _tile(m, 1024, group_rows)
    tn = _tile(n, max(LANES, MM_WEIGHT_ELEMS // k))
    return pl.pallas_call(
        _mmres_kernel,
        grid=(m // tm, n // tn),
        in_specs=[
            pl.BlockSpec((tm, k), lambda i, j: (i, 0), pipeline_mode=pl.Buffered(1)),
            pl.BlockSpec((k, tn), lambda i, j: (0, j)),
            pl.BlockSpec((tm, tn), lambda i, j: (i, j)),
            pl.BlockSpec((None, 1, tn), lambda i, j: ((i * tm) // group_rows, 0, j)),
        ],
        out_specs=pl.BlockSpec((tm, tn), lambda i, j: (i, j)),
        out_shape=jax.ShapeDtypeStruct((m, n), F32),
        compiler_params=_params("arbitrary", "arbitrary"),
        name=name,
    )(a, w, h, gate)


def _softmax_pv(s_parts, v_parts, sink_col):
    m = sink_col
    for s in s_parts:
        m = jnp.maximum(m, jnp.max(s, axis=-1, keepdims=True))
    es = [jnp.exp(s - m) for s in s_parts]
    den = jnp.exp(sink_col - m)
    for e in es:
        den = den + jnp.sum(e, axis=-1, keepdims=True)
    inv = 1.0 / den
    o = None
    for e, v in zip(es, v_parts):
        t = jnp.dot((e * inv).astype(BF16), v, preferred_element_type=F32)
        o = t if o is None else o + t
    return o


def _sink_column(sink_ref, kv, g_heads, rows):
    return jnp.concatenate([jnp.full((rows, 1), sink_ref[kv * g_heads + g], F32) for g in range(g_heads)], axis=0)


def _attn_lat_kernel(sink_ref, q_ref, k_ref, v_ref, kc_ref, vc_ref, o_ref, s_scr, p_scr, bias_scr, *,
                     s_len, c_len, tq, g_heads, hd, scale):
    kv = pl.program_id(1)
    qi = pl.program_id(2)
    blk = ATTN_BLOCK
    win = 3 * blk
    nblk = tq // blk
    rc = ATTN_SOFTMAX_ROWS
    kc = kc_ref[...]
    vc = vc_ref[...]
    rows = lax.broadcasted_iota(I32, (blk, win), 0)
    cols = lax.broadcasted_iota(I32, (blk, win), 1)
    rel = rows - cols + ATTN_WINDOW

    def body(bi, c):
        n = qi * nblk + bi
        start = pl.multiple_of(jnp.clip((n - 1) * blk, 0, s_len - win), blk)
        dist = rel + (n * blk - start)
        bias_scr[...] = jnp.where((dist >= 0) & (dist <= 2 * ATTN_WINDOW), 0.0, -jnp.inf)
        r0 = pl.multiple_of(bi * blk, blk)
        q4 = jnp.concatenate([q_ref[pl.ds(r0, blk), g * hd:(g + 1) * hd] for g in range(g_heads)], axis=0)
        s_scr[:, 0:win] = lax.dot_general(q4, k_ref[pl.ds(start, win), :], _NT, preferred_element_type=F32)
        s_scr[:, win:win + c_len] = lax.dot_general(q4, kc, _NT, preferred_element_type=F32)

        def softmax_rows(ci):
            rr = pl.ds(ci * rc, rc)
            qrow = pl.ds((ci * rc) % blk, rc)
            sink = sink_ref[kv * g_heads + (ci * rc) // blk]
            s_loc = s_scr[rr, 0:win] * scale + bias_scr[qrow, :]
            s_ctx = s_scr[rr, win:win + c_len] * scale
            m = jnp.maximum(jnp.maximum(jnp.max(s_loc, axis=-1, keepdims=True),
                                        jnp.max(s_ctx, axis=-1, keepdims=True)), sink)
            e_loc = jnp.exp(s_loc - m)
            e_ctx = jnp.exp(s_ctx - m)
            den = (jnp.sum(e_loc, axis=-1, keepdims=True) + jnp.sum(e_ctx, axis=-1, keepdims=True)
                   + jnp.exp(sink - m))
            inv = 1.0 / den
            p_scr[rr, 0:win] = (e_loc * inv).astype(p_scr.dtype)
            p_scr[rr, win:win + c_len] = (e_ctx * inv).astype(p_scr.dtype)

        for ci in range((g_heads * blk) // rc):
            softmax_rows(ci)
        o = (jnp.dot(p_scr[:, 0:win], v_ref[pl.ds(start, win), :], preferred_element_type=F32)
             + jnp.dot(p_scr[:, win:win + c_len], vc, preferred_element_type=F32))
        for g in range(g_heads):
            o_ref[pl.ds(r0, blk), g * hd:(g + 1) * hd] = o[g * blk:(g + 1) * blk].astype(o_ref.dtype)
        return c

    lax.fori_loop(0, nblk, body, 0)


def _attn_ctx_kernel(sink_ref, q_ref, kc_ref, vc_ref, o_ref, *, c_len, g_heads, hd, scale):
    kv = pl.program_id(1)
    sink_col = _sink_column(sink_ref, kv, g_heads, c_len)
    q4 = jnp.concatenate([q_ref[:, g * hd:(g + 1) * hd] for g in range(g_heads)], axis=0)
    kc = kc_ref[...]
    s = lax.dot_general(q4, kc, _NT, preferred_element_type=F32) * scale
    o = _softmax_pv([s], [vc_ref[...]], sink_col)
    for g in range(g_heads):
        o_ref[:, g * hd:(g + 1) * hd] = o[g * c_len:(g + 1) * c_len].astype(o_ref.dtype)


def _attention(qkv_lat, qkv_ctx, sinks, *, batch, s_len, c_len, n_heads, n_kv, hd):
    g_heads = n_heads // n_kv
    nq = n_heads * hd
    scale = hd ** -0.5
    tq = _tile(s_len, 1024)
    nqt = s_len // tq
    qw = g_heads * hd
    kcol = nq // hd
    vcol = kcol + n_kv
    smem = pl.BlockSpec(memory_space=pltpu.SMEM)
    lat = pl.pallas_call(
        functools.partial(_attn_lat_kernel, s_len=s_len, c_len=c_len, tq=tq, g_heads=g_heads, hd=hd, scale=scale),
        grid=(batch, n_kv, nqt),
        scratch_shapes=[
            pltpu.VMEM((g_heads * ATTN_BLOCK, 3 * ATTN_BLOCK + c_len), F32),
            pltpu.VMEM((g_heads * ATTN_BLOCK, 3 * ATTN_BLOCK + c_len), BF16),
            pltpu.VMEM((ATTN_BLOCK, 3 * ATTN_BLOCK), F32),
        ],
        in_specs=[
            smem,
            pl.BlockSpec((tq, qw), lambda b, h, i: (b * nqt + i, h)),
            pl.BlockSpec((s_len, hd), lambda b, h, i: (b, kcol + h)),
            pl.BlockSpec((s_len, hd), lambda b, h, i: (b, vcol + h)),
            pl.BlockSpec((c_len, hd), lambda b, h, i: (b, kcol + h)),
            pl.BlockSpec((c_len, hd), lambda b, h, i: (b, vcol + h)),
        ],
        out_specs=pl.BlockSpec((tq, qw), lambda b, h, i: (b * nqt + i, h)),
        out_shape=jax.ShapeDtypeStruct((batch * s_len, nq), BF16),
        compiler_params=_params("arbitrary", "arbitrary", "arbitrary"),
        name="attention_latent",
    )(sinks, qkv_lat, qkv_lat, qkv_lat, qkv_ctx, qkv_ctx)
    ctx = pl.pallas_call(
        functools.partial(_attn_ctx_kernel, c_len=c_len, g_heads=g_heads, hd=hd, scale=scale),
        grid=(batch, n_kv),
        in_specs=[
            smem,
            pl.BlockSpec((c_len, qw), lambda b, h: (b, h)),
            pl.BlockSpec((c_len, hd), lambda b, h: (b, kcol + h)),
            pl.BlockSpec((c_len, hd), lambda b, h: (b, vcol + h)),
        ],
        out_specs=pl.BlockSpec((c_len, qw), lambda b, h: (b, h)),
        out_shape=jax.ShapeDtypeStruct((batch * c_len, nq), BF16),
        compiler_params=_params("arbitrary", "arbitrary"),
        name="attention_context",
    )(sinks, qkv_ctx, qkv_ctx, qkv_ctx)
    return lat, ctx


def _ret_kernel(lg_ref, q_ref, k_ref, v_ref, g_ref, kc_ref, vc_ref, o_ref, s_scr, of_scr, *,
                n_seg, ts, c_len, dv):
    h = pl.program_id(1)
    j = pl.program_id(2)
    ch = RET_CHUNK
    nch = ts // ch

    def sweep(bwd):
        seg = 2 * n_seg - 1 - j if bwd else j
        lg = lg_ref[1 if bwd else 0, h]
        qi = lax.broadcasted_iota(I32, (ch, ch), 0)
        ki = lax.broadcasted_iota(I32, (ch, ch), 1)
        diff = ((ki - qi) if bwd else (qi - ki)).astype(F32)
        d_mask = jnp.exp(jnp.where(diff >= 0, diff * lg, -jnp.inf))
        pos = lax.broadcasted_iota(I32, (ch, 1), 0).astype(F32)
        fpos = ch - 1.0 - pos if bwd else pos
        xi = jnp.exp((fpos + 1.0) * lg)
        zeta = jnp.exp((ch - 1.0 - fpos) * lg)
        chunk_decay = jnp.exp(jnp.full((1, 1), ch, F32) * lg)

        @pl.when(j == (n_seg if bwd else 0))
        def _():
            cpos = lax.broadcasted_iota(I32, (c_len, 1), 0).astype(F32)
            w = jnp.exp((cpos if bwd else c_len - 1.0 - cpos) * lg)
            s_scr[...] = lax.dot_general((kc_ref[...] * w).astype(BF16), vc_ref[...].astype(BF16), _TN,
                                         preferred_element_type=F32)

        def chunk(ci, c):
            cc = nch - 1 - ci if bwd else ci
            r0 = pl.multiple_of(cc * ch, ch)
            rows = pl.ds(r0, ch)
            q = q_ref[rows, :].astype(BF16)
            kf = k_ref[rows, :]
            v = v_ref[rows, :].astype(BF16)
            s = s_scr[...]
            inner = lax.dot_general(q, kf.astype(BF16), _NT, preferred_element_type=F32) * d_mask
            o = jnp.dot(inner.astype(BF16), v, preferred_element_type=F32)
            o = o + jnp.dot(q, s.astype(BF16), preferred_element_type=F32) * xi
            s_scr[...] = chunk_decay * s + lax.dot_general((kf * zeta).astype(BF16), v, _TN,
                                                           preferred_element_type=F32)
            grow = pl.ds(pl.multiple_of(seg * ts + r0, ch), ch)
            if bwd:
                t = of_scr[grow, :] + o
                t = t * lax.rsqrt(jnp.sum(t * t, axis=-1, keepdims=True) * (1.0 / dv) + NORM_EPS)
                g = g_ref[rows, :]
                o_ref[rows, :] = (g * jax.nn.sigmoid(g) * t).astype(o_ref.dtype)
            else:
                of_scr[grow, :] = o
            return c

        lax.fori_loop(0, nch, chunk, 0, unroll=4)

    pl.when(j < n_seg)(lambda: sweep(False))
    pl.when(j >= n_seg)(lambda: sweep(True))


def _retention(p_lat, p_ctx, lg, *, batch, s_len, c_len, n_heads, dk, dv):
    qd = n_heads * dk
    vd = n_heads * dv
    ts = _tile(s_len, 1024)
    n_seg = s_len // ts

    def seg_of(j):
        return jnp.where(j >= n_seg, 2 * n_seg - 1 - j, j)

    def seg_bwd(j):
        return jnp.where(j >= n_seg, 2 * n_seg - 1 - j, n_seg - 1)

    kq = qd // dk
    kv_ = (2 * qd) // dv
    kg = (2 * qd + vd) // dv
    return pl.pallas_call(
        functools.partial(_ret_kernel, n_seg=n_seg, ts=ts, c_len=c_len, dv=dv),
        grid=(batch, n_heads, 2 * n_seg),
        in_specs=[
            pl.BlockSpec(memory_space=pltpu.SMEM),
            pl.BlockSpec((ts, dk), lambda b, h, j: (b * n_seg + seg_of(j), h)),
            pl.BlockSpec((ts, dk), lambda b, h, j: (b * n_seg + seg_of(j), kq + h)),
            pl.BlockSpec((ts, dv), lambda b, h, j: (b * n_seg + seg_of(j), kv_ + h)),
            pl.BlockSpec((ts, dv), lambda b, h, j: (b * n_seg + seg_bwd(j), kg + h)),
            pl.BlockSpec((c_len, dk), lambda b, h, j: (b, h)),
            pl.BlockSpec((c_len, dv), lambda b, h, j: (b, qd // dv + h)),
        ],
        out_specs=pl.BlockSpec((ts, dv), lambda b, h, j: (b * n_seg + seg_bwd(j), h)),
        out_shape=jax.ShapeDtypeStruct((batch * s_len, vd), BF16),
        scratch_shapes=[pltpu.VMEM((dk, dv), F32), pltpu.VMEM((s_len, dv), F32)],
        compiler_params=_params("arbitrary", "arbitrary", "arbitrary"),
        name="retention",
    )(lg, p_lat, p_lat, p_lat, p_lat, p_ctx, p_ctx)


def _router_kernel(*refs, n_src, nl_tiles, tm, d, n_exp):
    if n_src == 2:
        hl_ref, hc_ref, wm_ref, sh_ref, w_ref, b_ref, n_ref, idx_ref, w8_ref, r8_ref, cnt_ref, carry, nhi, nlo = refs
    else:
        hl_ref, wm_ref, sh_ref, w_ref, b_ref, n_ref, idx_ref, w8_ref, r8_ref, cnt_ref, carry, nhi, nlo = refs
        hc_ref = None
    i = pl.program_id(0)
    dh = d // 2

    @pl.when(i == 0)
    def _():
        carry[...] = jnp.zeros_like(carry)

    wm = wm_ref[...]
    sh = sh_ref[...]

    def fill(src_ref):
        def body(r, c):
            rows = pl.ds(pl.multiple_of(r * NORM_ROWS, NORM_ROWS), NORM_ROWS)
            n = _norm_rows(src_ref[rows, :], wm, sh, d)
            hi = n.astype(BF16)
            hi32 = hi.astype(F32)
            bits = lax.bitcast_convert_type(hi32, jnp.uint32)
            n_ref[rows, :] = (bits[:, :dh] >> 16) | (bits[:, dh:] & jnp.uint32(0xFFFF0000))
            nhi[rows, :] = hi
            nlo[rows, :] = (n - hi32).astype(BF16)
            return c

        lax.fori_loop(0, tm // NORM_ROWS, body, 0, unroll=NORM_UNROLL)

    if n_src == 2:
        pl.when(i < nl_tiles)(lambda: fill(hl_ref))
        pl.when(i >= nl_tiles)(lambda: fill(hc_ref))
    else:
        fill(hl_ref)

    w_hi = w_ref[0:n_exp, :]
    w_lo = w_ref[n_exp:2 * n_exp, :]
    x_hi = nhi[...]
    logits = (lax.dot_general(w_hi, x_hi, _NT, preferred_element_type=F32)
              + lax.dot_general(w_lo, x_hi, _NT, preferred_element_type=F32)
              + lax.dot_general(w_hi, nlo[...], _NT, preferred_element_type=F32)
              + b_ref[...])

    eidx = lax.broadcasted_iota(I32, (n_exp, tm), 0)
    row8 = lax.broadcasted_iota(I32, (8, tm), 0)
    rest = logits
    sel = jnp.zeros((n_exp, tm), jnp.bool_)
    picks = []
    idx8 = jnp.zeros((8, tm), I32)
    top1 = None
    for k in range(TOP_K):
        mx = jnp.max(rest, axis=0, keepdims=True)
        if k == 0:
            top1 = mx
        am = jnp.min(jnp.where(rest == mx, eidx, n_exp), axis=0, keepdims=True)
        pick = eidx == am
        picks.append(pick)
        sel = sel | pick
        idx8 = jnp.where(row8 == k, am, idx8)
        rest = jnp.where(pick, -jnp.inf, rest)
    e = jnp.where(sel, jnp.exp(logits - top1), 0.0)
    gates = e / jnp.sum(e, axis=0, keepdims=True)

    t0 = lax.broadcasted_iota(I32, (tm, tm), 0)
    t1 = lax.broadcasted_iota(I32, (tm, tm), 1)
    before = jnp.where(t0 < t1, 1.0, 0.0).astype(BF16)
    self_f = jnp.where(sel, 1.0, 0.0)
    rank = jnp.dot(self_f.astype(BF16), before, preferred_element_type=F32) + carry[:, 0:1]
    w8 = jnp.zeros((8, tm), F32)
    r8 = jnp.zeros((8, tm), F32)
    for k, pick in enumerate(picks):
        w8 = jnp.where(row8 == k, jnp.sum(jnp.where(pick, gates, 0.0), axis=0, keepdims=True), w8)
        r8 = jnp.where(row8 == k, jnp.sum(jnp.where(pick, rank, 0.0), axis=0, keepdims=True), r8)
    idx_ref[...] = idx8
    w8_ref[...] = w8
    r8_ref[...] = r8.astype(I32)
    new_carry = carry[...] + jnp.sum(self_f, axis=1, keepdims=True)
    carry[...] = new_carry
    cnt_ref[...] = new_carry


def _router(h_lat, h_ctx, wm, sh, rw, rb, *, group_rows, tm):
    ml, d = h_lat.shape
    n_exp = rw.shape[1]
    mc = 0 if h_ctx is None else h_ctx.shape[0]
    t = ml + mc
    assert ml % tm == 0 and mc % tm == 0 and group_rows % tm == 0
    nl_tiles = ml // tm
    n_groups_lat = ml // group_rows
    rw_t = rw.T
    w_hi = rw_t.astype(BF16)
    w_lo = (rw_t - w_hi.astype(F32)).astype(BF16)
    w3 = jnp.concatenate([w_hi, w_lo], axis=0)

    def grp(i):
        return jnp.where(i < nl_tiles, (i * tm) // group_rows, n_groups_lat)

    in_specs = [pl.BlockSpec((tm, d), lambda i: (jnp.minimum(i, nl_tiles - 1), 0))]
    args = [h_lat]
    if mc:
        in_specs.append(pl.BlockSpec((tm, d), lambda i: (jnp.maximum(i - nl_tiles, 0), 0)))
        args.append(h_ctx)
    in_specs += [
        pl.BlockSpec((None, 1, d), lambda i: (grp(i), 0, 0)),
        pl.BlockSpec((None, 1, d), lambda i: (grp(i), 0, 0)),
        pl.BlockSpec((2 * n_exp, d), lambda i: (0, 0)),
        pl.BlockSpec((n_exp, 1), lambda i: (0, 0)),
    ]
    args += [wm, sh, w3, rb.reshape(n_exp, 1)]
    return pl.pallas_call(
        functools.partial(_router_kernel, n_src=2 if mc else 1, nl_tiles=nl_tiles, tm=tm, d=d, n_exp=n_exp),
        grid=(t // tm,),
        in_specs=in_specs,
        out_specs=[
            pl.BlockSpec((tm, d // 2), lambda i: (i, 0)),
            pl.BlockSpec((8, tm), lambda i: (0, i)),
            pl.BlockSpec((8, tm), lambda i: (0, i)),
            pl.BlockSpec((8, tm), lambda i: (0, i)),
            pl.BlockSpec((n_exp, LANES), lambda i: (0, 0)),
        ],
        out_shape=[
            jax.ShapeDtypeStruct((t, d // 2), jnp.uint32),
            jax.ShapeDtypeStruct((8, t), I32),
            jax.ShapeDtypeStruct((8, t), F32),
            jax.ShapeDtypeStruct((8, t), I32),
            jax.ShapeDtypeStruct((n_exp, LANES), F32),
        ],
        scratch_shapes=[pltpu.VMEM((n_exp, LANES), F32), pltpu.VMEM((tm, d), BF16), pltpu.VMEM((tm, d), BF16)],
        compiler_params=_params("arbitrary"),
        name="moe_router",
    )(*args)


def _routing_tables(idx8, r8, cnt, *, tm, n_tiles):
    n_exp = cnt.shape[0]
    c = cnt[:, 0].astype(I32)
    pc = ((c + tm - 1) // tm) * tm
    cum = jnp.cumsum(pc)
    off = cum - pc
    idx4 = idx8[:TOP_K]
    off4 = jnp.sum(jnp.where(idx4[:, :, None] == jnp.arange(n_exp, dtype=I32), off, 0), axis=-1)
    pos4 = (off4 + r8[:TOP_K]).T.reshape(-1)
    extents = jnp.stack([cum[-1] // tm, jnp.asarray(n_tiles * tm, I32), jnp.asarray(idx8.shape[1], I32)]).astype(I32)
    tile_start = jnp.arange(n_tiles, dtype=I32) * tm
    tile_expert = jnp.minimum(jnp.sum(tile_start[:, None] >= cum[None, :], axis=1), n_exp - 1).astype(I32)
    return tile_expert, extents, pos4


def _moe_kernel(te_ref, nu_ref, pos_ref, n_hbm, w1_ref, b1_ref, w2_ref, b2_ref, o_ref, tok_ref, xbuf, sem, *,
                tm, f_dim, dh):
    i = pl.program_id(0)
    n_used = nu_ref[0]
    slot = i % GATHER_BUFS

    def row_copy(tile, r, sl):
        tok = tok_ref[tile * tm + r]
        return pltpu.make_async_copy(n_hbm.at[pl.ds(tok, 1), :], xbuf.at[sl, pl.ds(r, 1), :], sem.at[sl])

    def wait_rows(sl):
        pltpu.make_async_copy(n_hbm.at[pl.ds(0, tm), :], xbuf.at[sl], sem.at[sl]).wait()

    def start_rows(tile, sl):
        def body(rb, c):
            for u in range(DMA_UNROLL):
                row_copy(tile, rb * DMA_UNROLL + u, sl).start()
            return c

        lax.fori_loop(0, tm // DMA_UNROLL, body, 0)

    @pl.when(i == 0)
    def _():
        def zero(jb, c):
            for u in range(DMA_UNROLL):
                tok_ref[jb * DMA_UNROLL + u] = 0
            return c

        lax.fori_loop(0, nu_ref[1] // DMA_UNROLL, zero, 0)

        def invert(t, c):
            for k in range(TOP_K):
                tok_ref[pos_ref[t * TOP_K + k]] = t
            return c

        lax.fori_loop(0, nu_ref[2], invert, 0)
        for b in range(GATHER_BUFS - 1):
            start_rows(jnp.minimum(b, n_used - 1), b)

    def step(sl):
        ahead = jnp.minimum(i + GATHER_BUFS - 1, n_used - 1)
        for r in range(tm):
            row_copy(ahead, r, (sl + GATHER_BUFS - 1) % GATHER_BUFS).start()
        wait_rows(sl)
        words = xbuf[sl]
        x_lo = lax.bitcast_convert_type(words << 16, F32).astype(BF16)
        x_hi = lax.bitcast_convert_type(words & jnp.uint32(0xFFFF0000), F32).astype(BF16)
        gu = (jnp.dot(x_lo, w1_ref[0:dh, :], preferred_element_type=F32)
              + jnp.dot(x_hi, w1_ref[dh:2 * dh, :], preferred_element_type=F32) + b1_ref[...])
        gate = jnp.minimum(gu[:, :f_dim], SWIGLU_LIMIT)
        up = jnp.clip(gu[:, f_dim:], -SWIGLU_LIMIT, SWIGLU_LIMIT)
        act = (up + 1.0) * (gate * jax.nn.sigmoid(SWIGLU_ALPHA * gate))
        o_ref[...] = jnp.dot(act.astype(BF16), w2_ref[...], preferred_element_type=F32) + b2_ref[...]

        @pl.when(i == n_used - 1)
        def _():
            for ds in range(1, GATHER_BUFS):
                wait_rows((sl + ds) % GATHER_BUFS)

    for s in range(GATHER_BUFS):
        pl.when((i < n_used) & (slot == s))(functools.partial(step, s))

    @pl.when(i >= n_used)
    def _():
        o_ref[...] = jnp.zeros_like(o_ref)


def _experts(n_packed, tile_expert, n_used, pos4, w1, b1, w2, b2, *, layer, tm, n_tiles):
    n_layers, n_exp, d, f2 = w1.shape
    f_dim = f2 // 2
    dh = n_packed.shape[1]
    grid_spec = pltpu.PrefetchScalarGridSpec(
        num_scalar_prefetch=3,
        grid=(n_tiles,),
        in_specs=[
            pl.BlockSpec(memory_space=pl.ANY),
            pl.BlockSpec((None, None, d, f2), lambda i, te, nu, pos: (layer, te[i], 0, 0)),
            pl.BlockSpec((None, None, 1, f2), lambda i, te, nu, pos: (layer, te[i], 0, 0)),
            pl.BlockSpec((None, None, f_dim, d), lambda i, te, nu, pos: (layer, te[i], 0, 0)),
            pl.BlockSpec((None, None, 1, d), lambda i, te, nu, pos: (layer, te[i], 0, 0)),
        ],
        out_specs=pl.BlockSpec((tm, d), lambda i, te, nu, pos: (i, 0)),
        scratch_shapes=[pltpu.SMEM((n_tiles * tm,), I32), pltpu.VMEM((GATHER_BUFS, tm, dh), jnp.uint32),
                        pltpu.SemaphoreType.DMA((GATHER_BUFS,))],
    )
    return pl.pallas_call(
        functools.partial(_moe_kernel, tm=tm, f_dim=f_dim, dh=dh),
        grid_spec=grid_spec,
        out_shape=jax.ShapeDtypeStruct((n_tiles * tm, d), F32),
        compiler_params=_params("arbitrary"),
        name="moe_experts",
    )(tile_expert, n_used, pos4, n_packed, w1, b1.reshape(n_layers, n_exp, 1, f2), w2,
      b2.reshape(n_layers, n_exp, 1, d))


def _combine_kernel(*refs, tc, d, tok_base, final):
    if final:
        pos_ref, ys_hbm, h_ref, g_ref, w_ref, fw_ref, o_ref, buf, sem = refs
    else:
        pos_ref, ys_hbm, h_ref, g_ref, w_ref, o_ref, buf, sem = refs
        fw_ref = None
    i = pl.program_id(0)
    nt = pl.num_programs(0)
    slot = i % 2

    def start_rows(tile, r0, sl):
        for u in range(NORM_ROWS):
            base = (tok_base + tile * tc + r0 + u) * TOP_K
            for k in range(TOP_K):
                pltpu.make_async_copy(ys_hbm.at[pl.ds(pos_ref[base + k], 1), :],
                                      buf.at[sl, pl.ds(k * tc + r0 + u, 1), :], sem.at[sl]).start()

    def wait_rows(sl):
        pltpu.make_async_copy(ys_hbm.at[pl.ds(0, TOP_K * tc), :], buf.at[sl], sem.at[sl]).wait()

    @pl.when(i == 0)
    def _():
        def first(r, c):
            start_rows(0, r * NORM_ROWS, 0)
            return c

        lax.fori_loop(0, tc // NORM_ROWS, first, 0)

    gate = g_ref[...]
    fw = None if fw_ref is None else fw_ref[...]
    nxt = jnp.minimum(i + 1, nt - 1)

    def step(sl):
        wait_rows(sl)
        for r0 in range(0, tc, NORM_ROWS):
            start_rows(nxt, r0, 1 - sl)
            wk = w_ref[r0:r0 + NORM_ROWS, :]
            y = wk[:, 0:1] * buf[sl, r0:r0 + NORM_ROWS, :]
            for k in range(1, TOP_K):
                y = y + wk[:, k:k + 1] * buf[sl, k * tc + r0:k * tc + r0 + NORM_ROWS, :]
            out = h_ref[r0:r0 + NORM_ROWS, :] + gate * y
            if final:
                ms = jnp.sum(out * out, axis=-1, keepdims=True) * (1.0 / d)
                out = out * lax.rsqrt(ms + NORM_EPS) * fw
            o_ref[r0:r0 + NORM_ROWS, :] = out

        @pl.when(i == nt - 1)
        def _():
            wait_rows(1 - sl)

    pl.when(slot == 0)(lambda: step(0))
    pl.when(slot == 1)(lambda: step(1))


def _combine(ys, pos4, w4, h, gate, *, group_rows, tok_base, final_w=None):
    m, d = h.shape
    tc = _tile(m, 128)
    assert group_rows % tc == 0 and tok_base % tc == 0
    final = final_w is not None
    in_specs = [
        pl.BlockSpec(memory_space=pl.ANY),
        pl.BlockSpec((tc, d), lambda i, pos: (i, 0)),
        pl.BlockSpec((None, 1, d), lambda i, pos: ((i * tc) // group_rows, 0, 0)),
        pl.BlockSpec((tc, TOP_K), lambda i, pos: (tok_base // tc + i, 0)),
    ]
    args = [pos4, ys, h, gate, w4]
    if final:
        in_specs.append(pl.BlockSpec((1, d), lambda i, pos: (0, 0)))
        args.append(final_w.reshape(1, d))
    grid_spec = pltpu.PrefetchScalarGridSpec(
        num_scalar_prefetch=1,
        grid=(m // tc,),
        in_specs=in_specs,
        out_specs=pl.BlockSpec((tc, d), lambda i, pos: (i, 0)),
        scratch_shapes=[pltpu.VMEM((2, TOP_K * tc, d), F32), pltpu.SemaphoreType.DMA((2,))],
    )
    return pl.pallas_call(
        functools.partial(_combine_kernel, tc=tc, d=d, tok_base=tok_base, final=final),
        grid_spec=grid_spec,
        out_shape=jax.ShapeDtypeStruct((m, d), F32),
        compiler_params=_params("arbitrary"),
        name="moe_combine",
    )(*args)


def _moe(h_lat, h_ctx, wm, sh, gate, rw, rb, w1, b1, w2, b2, *, layer, s_len, final_w=None):
    ml = h_lat.shape[0]
    mc = 0 if h_ctx is None else h_ctx.shape[0]
    t = ml + mc
    n_exp = rw.shape[1]
    nb = ml // s_len
    tm_r = _tile(math.gcd(ml, mc) if mc else ml, 256)
    n_packed, idx8, w8, r8, cnt = _router(h_lat, h_ctx, wm, sh, rw, rb, group_rows=s_len, tm=tm_r)
    tm = 256
    n_tiles = (TOP_K * t) // tm + n_exp
    tile_expert, n_used, pos4 = _routing_tables(idx8, r8, cnt, tm=tm, n_tiles=n_tiles)
    w4 = w8[:TOP_K].T
    ys = _experts(n_packed, tile_expert, n_used, pos4, w1, b1, w2, b2, layer=layer, tm=tm, n_tiles=n_tiles)
    new_lat = _combine(ys, pos4, w4, h_lat, gate[:nb], group_rows=s_len, tok_base=0, final_w=final_w)
    new_ctx = None
    if mc:
        new_ctx = _combine(ys, pos4, w4, h_ctx, gate[nb:nb + 1], group_rows=mc, tok_base=ml)
    return new_lat, new_ctx


def _axial_tables(s_len, hd):
    half = hd // 2
    inv = 1.0 / (ROPE_BASE ** (jnp.arange(0, half, 2, dtype=F32) / half))
    t = jnp.arange(s_len)
    rows = (t // GRID_W).astype(F32)[:, None] * inv[None, :]
    cols = (t % GRID_W).astype(F32)[:, None] * inv[None, :]
    cos = jnp.concatenate([jnp.cos(rows), jnp.cos(rows), jnp.cos(cols), jnp.cos(cols)], axis=-1)
    sin = jnp.concatenate([-jnp.sin(rows), jnp.sin(rows), -jnp.sin(cols), jnp.sin(cols)], axis=-1)
    return cos, sin


def _retention_tables(s_len, dk):
    theta = 1.0 / (ROPE_BASE ** jnp.linspace(0.0, 1.0, dk // 2, dtype=F32))
    ang = jnp.arange(s_len, dtype=F32)[:, None] * theta[None, :]
    return jnp.cos(ang), jnp.sin(ang)


def kernel(x, c, ctx, c_ctx, mod_w, mod_b, norm1_w, norm2_w, attn_w_qkv, attn_w_o, attn_sinks, ret_w_in,
           ret_w_out, ret_a_fwd, ret_a_bwd, router_w, router_b, exp_w1, exp_b1, exp_w2, exp_b2, final_norm_w):
    batch, s_len, d = x.shape
    c_len = ctx.shape[1]
    depth = mod_w.shape[0]
    assert depth == 2 and attn_w_qkv.shape[0] == 1 and ret_w_in.shape[0] == 1
    n_heads = attn_sinks.shape[1]
    hd = d // n_heads
    n_kv = (attn_w_qkv.shape[2] // hd - n_heads) // 2
    r_heads = ret_a_fwd.shape[1]
    dk = d // r_heads
    dv = 2 * dk
    qd, vd = r_heads * dk, r_heads * dv
    assert hd == LANES and dk == 2 * LANES
    ml, mc = batch * s_len, batch * c_len

    n_rows = -(-(batch + 1) // 8) * 8
    cvec = jnp.zeros((n_rows, d), F32).at[:batch].set(c).at[batch].set(c_ctx)
    mods = _modulation(cvec, mod_w, mod_b)

    def mod_tables(i):
        sh1, sc1, g1, sh2, sc2, g2 = [a[:batch + 1, None, :] for a in jnp.split(mods[i], 6, axis=-1)]
        wm1 = norm1_w[i][None, None, :] * (1.0 + sc1)
        wm2 = norm2_w[i][None, None, :] * (1.0 + sc2)
        return wm1, sh1, g1, wm2, sh2, g2

    x2 = x.reshape(ml, d)
    ctx2 = ctx.reshape(mc, d)

    wm1, sh1, g1, wm2, sh2, g2 = mod_tables(0)
    w_qkv = attn_w_qkv[0].astype(BF16)
    nq = n_heads * hd
    qkv_lat = _norm_matmul(x2, wm1[:batch], sh1[:batch], w_qkv, group_rows=s_len, out_dtype=BF16,
                           rot="axial", tabs=_axial_tables(s_len, hd), rot_cols=nq + n_kv * hd,
                           name="qkv_latent")
    qkv_ctx = _norm_matmul(ctx2, wm1[batch:], sh1[batch:], w_qkv, group_rows=mc, out_dtype=BF16,
                           name="qkv_context")
    o_lat, o_ctx = _attention(qkv_lat, qkv_ctx, attn_sinks[0], batch=batch, s_len=s_len, c_len=c_len,
                              n_heads=n_heads, n_kv=n_kv, hd=hd)
    w_o = attn_w_o[0].astype(BF16)
    h_lat = _matmul_residual(o_lat, w_o, x2, g1[:batch], group_rows=s_len, name="attn_out_latent")
    h_ctx = _matmul_residual(o_ctx, w_o, ctx2, g1[batch:], group_rows=mc, name="attn_out_context")
    w1_all = exp_w1.astype(BF16)
    w2_all = exp_w2.astype(BF16)
    h_lat, h_ctx = _moe(h_lat, h_ctx, wm2, sh2, g2, router_w[0], router_b[0], w1_all, exp_b1, w2_all, exp_b2,
                        layer=0, s_len=s_len)

    wm1, sh1, g1, wm2, sh2, g2 = mod_tables(1)
    w_in = ret_w_in[0].astype(BF16)
    k_scale = dk ** -0.5
    p_lat = _norm_matmul(h_lat, wm1[:batch], sh1[:batch], w_in, group_rows=s_len, out_dtype=F32,
                         rot="half", tabs=_retention_tables(s_len, dk), rot_cols=2 * qd,
                         k_lo=qd, k_hi=2 * qd, k_scale=k_scale, name="ret_in_latent")
    p_ctx = _norm_matmul(h_ctx, wm1[batch:], sh1[batch:], w_in, group_rows=mc, out_dtype=F32,
                         col_off=qd, n_out=qd + vd, k_lo=qd, k_hi=2 * qd, k_scale=k_scale,
                         name="ret_in_context")
    lg = jnp.stack([jnp.log1p(-jnp.exp(ret_a_fwd[0].astype(F32))), jnp.log1p(-jnp.exp(ret_a_bwd[0].astype(F32)))])
    r_lat = _retention(p_lat, p_ctx, lg, batch=batch, s_len=s_len, c_len=c_len, n_heads=r_heads, dk=dk, dv=dv)
    h_lat = _matmul_residual(r_lat, ret_w_out[0].astype(BF16), h_lat, g1[:batch], group_rows=s_len,
                             name="ret_out_latent")
    out, _ = _moe(h_lat, None, wm2, sh2, g2, router_w[1], router_b[1], w1_all, exp_b1, w2_all, exp_b2,
                  layer=1, s_len=s_len, final_w=final_norm_w)
    return out.reshape(batch, s_len, d)
```

```python
import functools
import math

import jax
import jax.numpy as jnp
from jax import lax
from jax.experimental import pallas as pl
from jax.experimental.pallas import tpu as pltpu

F32 = jnp.float32
BF16 = jnp.bfloat16
I32 = jnp.int32

GRID_W = 64
ATTN_WINDOW = 128
ATTN_BLOCK = 128
ATTN_SOFTMAX_ROWS = 32
RET_CHUNK = 128
ROPE_BASE = 10000.0
TOP_K = 4
SWIGLU_LIMIT = 7.0
SWIGLU_ALPHA = 1.702
NORM_EPS = 1e-6

LANES = 128
NORM_ROWS = 16
NORM_UNROLL = 4
DMA_UNROLL = 32
GATHER_BUFS = 3
INVERT_TOKENS = 4
MM_WEIGHT_ELEMS = 4 * 1024 * 1024
MM_OUT_BLOCK_BYTES = 4 * 1024 * 1024
MM_DOUBLE_BUFFER_BYTES = 40 * 1024 * 1024
NORM_STREAM_ROWS = 128
VMEM_LIMIT = 58 * 1024 * 1024

_NT = (((1,), (1,)), ((), ()))
_TN = (((0,), (0,)), ((), ()))


def _params(*sem):
    return pltpu.CompilerParams(dimension_semantics=sem, vmem_limit_bytes=VMEM_LIMIT)


def _tile(dim, pref, *also):
    for t in range(min(dim, pref), 7, -1):
        if t % 8 == 0 and not any(e % t for e in (dim,) + also):
            return t
    raise ValueError((dim, pref, also))


def _mod_kernel(s_ref, w_ref, b_ref, o_ref):
    s = s_ref[...]
    s = s * jax.nn.sigmoid(s)
    o_ref[...] = jnp.dot(s.astype(BF16), w_ref[...].astype(BF16), preferred_element_type=F32) + b_ref[...]


def _modulation(cvec, mod_w, mod_b):
    nl, d, n = mod_w.shape
    r = cvec.shape[0]
    tn = _tile(n, 512)
    return pl.pallas_call(
        _mod_kernel,
        grid=(nl, n // tn),
        in_specs=[
            pl.BlockSpec((r, d), lambda l, j: (0, 0)),
            pl.BlockSpec((None, d, tn), lambda l, j: (l, 0, j)),
            pl.BlockSpec((None, 1, tn), lambda l, j: (l, 0, j)),
        ],
        out_specs=pl.BlockSpec((None, r, tn), lambda l, j: (l, 0, j)),
        out_shape=jax.ShapeDtypeStruct((nl, r, n), F32),
        compiler_params=_params("arbitrary", "arbitrary"),
        name="modulation",
    )(cvec, mod_w, mod_b.reshape(nl, 1, n))


def _norm_rows(x, wm, sh, d):
    ms = jnp.sum(x * x, axis=-1, keepdims=True) * (1.0 / d)
    return x * lax.rsqrt(ms + NORM_EPS) * wm + sh


def _nmm_kernel(*refs, d, tm, tn, col_off, rot, rot_cols, k_lo, k_hi, k_scale):
    if rot is None:
        x_hbm, wm_ref, sh_ref, w_ref, o_ref, n_scr, xbuf, xsem = refs
        cos_ref = sin_ref = None
    else:
        x_hbm, wm_ref, sh_ref, w_ref, cos_ref, sin_ref, o_ref, n_scr, xbuf, xsem = refs
    i = pl.program_id(0)
    j = pl.program_id(1)
    xr = xbuf.shape[1]
    n_chunks = tm // xr

    @pl.when(j == 0)
    def _():
        wm = wm_ref[...]
        sh = sh_ref[...]

        def x_copy(c, sl):
            return pltpu.make_async_copy(x_hbm.at[pl.ds(i * tm + c * xr, xr), :], xbuf.at[sl], xsem.at[sl])

        x_copy(0, 0).start()

        def chunk(c, carry):
            sl = c % 2

            @pl.when(c + 1 < n_chunks)
            def _():
                x_copy(c + 1, 1 - sl).start()

            x_copy(c, sl).wait()
            for u in range(xr // NORM_ROWS):
                n_scr[pl.ds(pl.multiple_of(c * xr + u * NORM_ROWS, NORM_ROWS), NORM_ROWS), :] = _norm_rows(
                    xbuf[sl, u * NORM_ROWS:(u + 1) * NORM_ROWS, :], wm, sh, d).astype(BF16)
            return carry

        lax.fori_loop(0, n_chunks, chunk, 0)

    acc = jnp.dot(n_scr[...], w_ref[...], preferred_element_type=F32)
    col0 = j * tn + col_off
    if k_scale is not None:
        acc = acc * jnp.where((col0 >= k_lo) & (col0 < k_hi), k_scale, 1.0).astype(F32)

    if rot is None:
        o_ref[...] = acc.astype(o_ref.dtype)
        return

    cos = cos_ref[...]
    sin = sin_ref[...]
    if rot == "axial":
        lane = lax.broadcasted_iota(I32, (tm, LANES), 1)
        first = (lane & 32) == 0
        for hh in range(tn // LANES):
            a = acc[:, hh * LANES:(hh + 1) * LANES]
            partner = jnp.where(first, pltpu.roll(a, LANES - 32, 1), pltpu.roll(a, 32, 1))
            o_ref[:, hh * LANES:(hh + 1) * LANES] = (a * cos + partner * sin).astype(o_ref.dtype)
    else:
        for hh in range(tn // (2 * LANES)):
            c0 = hh * 2 * LANES
            x1 = acc[:, c0:c0 + LANES]
            x2 = acc[:, c0 + LANES:c0 + 2 * LANES]
            o_ref[:, c0:c0 + LANES] = (x1 * cos - x2 * sin).astype(o_ref.dtype)
            o_ref[:, c0 + LANES:c0 + 2 * LANES] = (x1 * sin + x2 * cos).astype(o_ref.dtype)


def _norm_matmul(h, wm, sh, w, *, group_rows, out_dtype, col_off=0, n_out=None, rot=None, tabs=None,
                 rot_cols=0, k_lo=0, k_hi=0, k_scale=None, name="norm_matmul"):
    m, d = h.shape
    n_out = w.shape[1] - col_off if n_out is None else n_out
    tm = _tile(m, 1024, group_rows)
    tn = min(MM_WEIGHT_ELEMS // d, MM_OUT_BLOCK_BYTES // (tm * jnp.dtype(out_dtype).itemsize))
    tn = _tile(n_out, tn, col_off, rot_cols, k_lo, k_hi)
    xr = _tile(tm, NORM_STREAM_ROWS)
    in_specs = [
        pl.BlockSpec(memory_space=pl.ANY),
        pl.BlockSpec((None, 1, d), lambda i, j: ((i * tm) // group_rows, 0, 0)),
        pl.BlockSpec((None, 1, d), lambda i, j: ((i * tm) // group_rows, 0, 0)),
        pl.BlockSpec((d, tn), lambda i, j: (0, j + col_off // tn)),
    ]
    args = [h, wm, sh, w]
    if rot is not None:
        cos, sin = tabs
        s_pos = cos.shape[0]
        assert s_pos % tm == 0 and rot_cols % tn == 0
        nper = s_pos // tm
        cos2 = jnp.stack([cos, jnp.ones_like(cos)])
        sin2 = jnp.stack([sin, jnp.zeros_like(sin)])
        in_specs += [pl.BlockSpec((None, tm, LANES),
                                  lambda i, j: (jnp.where(j * tn + col_off < rot_cols, 0, 1), i % nper, 0))] * 2
        args += [cos2, sin2]
    kern = functools.partial(_nmm_kernel, d=d, tm=tm, tn=tn, col_off=col_off, rot=rot, rot_cols=rot_cols,
                             k_lo=k_lo, k_hi=k_hi, k_scale=k_scale)
    return pl.pallas_call(
        kern,
        grid=(m // tm, n_out // tn),
        in_specs=in_specs,
        out_specs=pl.BlockSpec((tm, tn), lambda i, j: (i, j)),
        out_shape=jax.ShapeDtypeStruct((m, n_out), out_dtype),
        scratch_shapes=[pltpu.VMEM((tm, d), BF16), pltpu.VMEM((2, xr, d), F32), pltpu.SemaphoreType.DMA((2,))],
        compiler_params=_params("arbitrary", "arbitrary"),
        name=name,
    )(*args)


def _mmres_kernel(a_ref, w_ref, h_ref, g_ref, o_ref):
    acc = jnp.dot(a_ref[...], w_ref[...], preferred_element_type=F32)
    o_ref[...] = h_ref[...] + g_ref[...] * acc


def _matmul_residual(a, w, h, gate, *, group_rows, name="matmul_residual"):
    m, k = a.shape
    n = w.shape[1]
    tm = _tile(m, MM_WEIGHT_ELEMS // k, group_rows)
    tn = _tile(n, max(LANES, MM_WEIGHT_ELEMS // k))
    two_copies = 2 * (tm * k + k * tn) * a.dtype.itemsize + 4 * tm * tn * 4
    a_mode = {} if two_copies <= MM_DOUBLE_BUFFER_BYTES else {"pipeline_mode": pl.Buffered(1)}
    return pl.pallas_call(
        _mmres_kernel,
        grid=(m // tm, n // tn),
        in_specs=[
            pl.BlockSpec((tm, k), lambda i, j: (i, 0), **a_mode),
            pl.BlockSpec((k, tn), lambda i, j: (0, j)),
            pl.BlockSpec((tm, tn), lambda i, j: (i, j)),
            pl.BlockSpec((None, 1, tn), lambda i, j: ((i * tm) // group_rows, 0, j)),
        ],
        out_specs=pl.BlockSpec((tm, tn), lambda i, j: (i, j)),
        out_shape=jax.ShapeDtypeStruct((m, n), F32),
        compiler_params=_params("arbitrary", "arbitrary"),
        name=name,
    )(a, w, h, gate)


def _softmax_pv(s_parts, v_parts, sink_col):
    m = sink_col
    for s in s_parts:
        m = jnp.maximum(m, jnp.max(s, axis=-1, keepdims=True))
    es = [jnp.exp(s - m) for s in s_parts]
    den = jnp.exp(sink_col - m)
    for e in es:
        den = den + jnp.sum(e, axis=-1, keepdims=True)
    inv = 1.0 / den
    o = None
    for e, v in zip(es, v_parts):
        t = jnp.dot((e * inv).astype(BF16), v, preferred_element_type=F32)
        o = t if o is None else o + t
    return o


def _sink_column(sink_ref, kv, g_heads, rows):
    return jnp.concatenate([jnp.full((rows, 1), sink_ref[kv * g_heads + g], F32) for g in range(g_heads)], axis=0)


def _attn_lat_kernel(sink_ref, q_ref, k_ref, v_ref, kc_ref, vc_ref, o_ref, s_scr, p_scr, bias_scr, *,
                     s_len, c_len, tq, g_heads, hd, scale):
    kv = pl.program_id(1)
    qi = pl.program_id(2)
    blk = ATTN_BLOCK
    win = 3 * blk
    nblk = tq // blk
    rc = ATTN_SOFTMAX_ROWS
    kc = kc_ref[...]
    vc = vc_ref[...]
    rows = lax.broadcasted_iota(I32, (blk, win), 0)
    cols = lax.broadcasted_iota(I32, (blk, win), 1)
    rel = rows - cols + ATTN_WINDOW

    def body(bi, c):
        n = qi * nblk + bi
        start = pl.multiple_of(jnp.clip((n - 1) * blk, 0, s_len - win), blk)
        dist = rel + (n * blk - start)
        bias_scr[...] = jnp.where((dist >= 0) & (dist <= 2 * ATTN_WINDOW), 0.0, -jnp.inf)
        r0 = pl.multiple_of(bi * blk, blk)
        q4 = jnp.concatenate([q_ref[pl.ds(r0, blk), g * hd:(g + 1) * hd] for g in range(g_heads)], axis=0)
        s_scr[:, 0:win] = lax.dot_general(q4, k_ref[pl.ds(start, win), :], _NT, preferred_element_type=F32)
        s_scr[:, win:win + c_len] = lax.dot_general(q4, kc, _NT, preferred_element_type=F32)

        def softmax_rows(ci):
            rr = pl.ds(ci * rc, rc)
            qrow = pl.ds((ci * rc) % blk, rc)
            sink = sink_ref[kv * g_heads + (ci * rc) // blk]
            s_loc = s_scr[rr, 0:win] * scale + bias_scr[qrow, :]
            s_ctx = s_scr[rr, win:win + c_len] * scale
            m = jnp.maximum(jnp.maximum(jnp.max(s_loc, axis=-1, keepdims=True),
                                        jnp.max(s_ctx, axis=-1, keepdims=True)), sink)
            e_loc = jnp.exp(s_loc - m)
            e_ctx = jnp.exp(s_ctx - m)
            den = (jnp.sum(e_loc, axis=-1, keepdims=True) + jnp.sum(e_ctx, axis=-1, keepdims=True)
                   + jnp.exp(sink - m))
            inv = 1.0 / den
            p_scr[rr, 0:win] = (e_loc * inv).astype(p_scr.dtype)
            p_scr[rr, win:win + c_len] = (e_ctx * inv).astype(p_scr.dtype)

        for ci in range((g_heads * blk) // rc):
            softmax_rows(ci)
        o = (jnp.dot(p_scr[:, 0:win], v_ref[pl.ds(start, win), :], preferred_element_type=F32)
             + jnp.dot(p_scr[:, win:win + c_len], vc, preferred_element_type=F32))
        for g in range(g_heads):
            o_ref[pl.ds(r0, blk), g * hd:(g + 1) * hd] = o[g * blk:(g + 1) * blk].astype(o_ref.dtype)
        return c

    lax.fori_loop(0, nblk, body, 0)


def _attn_ctx_kernel(sink_ref, q_ref, kc_ref, vc_ref, o_ref, *, c_len, g_heads, hd, scale):
    kv = pl.program_id(1)
    sink_col = _sink_column(sink_ref, kv, g_heads, c_len)
    q4 = jnp.concatenate([q_ref[:, g * hd:(g + 1) * hd] for g in range(g_heads)], axis=0)
    kc = kc_ref[...]
    s = lax.dot_general(q4, kc, _NT, preferred_element_type=F32) * scale
    o = _softmax_pv([s], [vc_ref[...]], sink_col)
    for g in range(g_heads):
        o_ref[:, g * hd:(g + 1) * hd] = o[g * c_len:(g + 1) * c_len].astype(o_ref.dtype)


def _attention(qkv_lat, qkv_ctx, sinks, *, batch, s_len, c_len, n_heads, n_kv, hd):
    g_heads = n_heads // n_kv
    nq = n_heads * hd
    scale = hd ** -0.5
    tq = _tile(s_len, 1024)
    nqt = s_len // tq
    qw = g_heads * hd
    kcol = nq // hd
    vcol = kcol + n_kv
    smem = pl.BlockSpec(memory_space=pltpu.SMEM)
    lat = pl.pallas_call(
        functools.partial(_attn_lat_kernel, s_len=s_len, c_len=c_len, tq=tq, g_heads=g_heads, hd=hd, scale=scale),
        grid=(batch, n_kv, nqt),
        scratch_shapes=[
            pltpu.VMEM((g_heads * ATTN_BLOCK, 3 * ATTN_BLOCK + c_len), F32),
            pltpu.VMEM((g_heads * ATTN_BLOCK, 3 * ATTN_BLOCK + c_len), BF16),
            pltpu.VMEM((ATTN_BLOCK, 3 * ATTN_BLOCK), F32),
        ],
        in_specs=[
            smem,
            pl.BlockSpec((tq, qw), lambda b, h, i: (b * nqt + i, h)),
            pl.BlockSpec((s_len, hd), lambda b, h, i: (b, kcol + h)),
            pl.BlockSpec((s_len, hd), lambda b, h, i: (b, vcol + h)),
            pl.BlockSpec((c_len, hd), lambda b, h, i: (b, kcol + h)),
            pl.BlockSpec((c_len, hd), lambda b, h, i: (b, vcol + h)),
        ],
        out_specs=pl.BlockSpec((tq, qw), lambda b, h, i: (b * nqt + i, h)),
        out_shape=jax.ShapeDtypeStruct((batch * s_len, nq), BF16),
        compiler_params=_params("arbitrary", "arbitrary", "arbitrary"),
        name="attention_latent",
    )(sinks, qkv_lat, qkv_lat, qkv_lat, qkv_ctx, qkv_ctx)
    ctx = pl.pallas_call(
        functools.partial(_attn_ctx_kernel, c_len=c_len, g_heads=g_heads, hd=hd, scale=scale),
        grid=(batch, n_kv),
        in_specs=[
            smem,
            pl.BlockSpec((c_len, qw), lambda b, h: (b, h)),
            pl.BlockSpec((c_len, hd), lambda b, h: (b, kcol + h)),
            pl.BlockSpec((c_len, hd), lambda b, h: (b, vcol + h)),
        ],
        out_specs=pl.BlockSpec((c_len, qw), lambda b, h: (b, h)),
        out_shape=jax.ShapeDtypeStruct((batch * c_len, nq), BF16),
        compiler_params=_params("arbitrary", "arbitrary"),
        name="attention_context",
    )(sinks, qkv_ctx, qkv_ctx, qkv_ctx)
    return lat, ctx


def _ret_kernel(lg_ref, q_ref, k_ref, v_ref, g_ref, kc_ref, vc_ref, o_ref, s_scr, of_scr, *,
                n_seg, ts, c_len, dv):
    h = pl.program_id(1)
    j = pl.program_id(2)
    ch = RET_CHUNK
    nch = ts // ch

    def sweep(bwd):
        seg = 2 * n_seg - 1 - j if bwd else j
        lg = lg_ref[1 if bwd else 0, h]
        qi = lax.broadcasted_iota(I32, (ch, ch), 0)
        ki = lax.broadcasted_iota(I32, (ch, ch), 1)
        diff = ((ki - qi) if bwd else (qi - ki)).astype(F32)
        d_mask = jnp.exp(jnp.where(diff >= 0, diff * lg, -jnp.inf))
        pos = lax.broadcasted_iota(I32, (ch, 1), 0).astype(F32)
        fpos = ch - 1.0 - pos if bwd else pos
        xi = jnp.exp((fpos + 1.0) * lg)
        zeta = jnp.exp((ch - 1.0 - fpos) * lg)
        chunk_decay = jnp.exp(jnp.full((1, 1), ch, F32) * lg)

        @pl.when(j == (n_seg if bwd else 0))
        def _():
            cpos = lax.broadcasted_iota(I32, (c_len, 1), 0).astype(F32)
            w = jnp.exp((cpos if bwd else c_len - 1.0 - cpos) * lg)
            s_scr[...] = lax.dot_general((kc_ref[...] * w).astype(BF16), vc_ref[...].astype(BF16), _TN,
                                         preferred_element_type=F32)

        def chunk(ci, c):
            cc = nch - 1 - ci if bwd else ci
            r0 = pl.multiple_of(cc * ch, ch)
            rows = pl.ds(r0, ch)
            q = q_ref[rows, :].astype(BF16)
            kf = k_ref[rows, :]
            v = v_ref[rows, :].astype(BF16)
            s = s_scr[...]
            inner = lax.dot_general(q, kf.astype(BF16), _NT, preferred_element_type=F32) * d_mask
            o = jnp.dot(inner.astype(BF16), v, preferred_element_type=F32)
            o = o + jnp.dot(q, s.astype(BF16), preferred_element_type=F32) * xi
            s_scr[...] = chunk_decay * s + lax.dot_general((kf * zeta).astype(BF16), v, _TN,
                                                           preferred_element_type=F32)
            grow = pl.ds(pl.multiple_of(seg * ts + r0, ch), ch)
            if bwd:
                t = of_scr[grow, :] + o
                t = t * lax.rsqrt(jnp.sum(t * t, axis=-1, keepdims=True) * (1.0 / dv) + NORM_EPS)
                g = g_ref[rows, :]
                o_ref[rows, :] = (g * jax.nn.sigmoid(g) * t).astype(o_ref.dtype)
            else:
                of_scr[grow, :] = o
            return c

        lax.fori_loop(0, nch, chunk, 0, unroll=8)

    pl.when(j < n_seg)(lambda: sweep(False))
    pl.when(j >= n_seg)(lambda: sweep(True))


def _retention(p_lat, p_ctx, lg, *, batch, s_len, c_len, n_heads, dk, dv):
    qd = n_heads * dk
    vd = n_heads * dv
    ts = _tile(s_len, 1024)
    n_seg = s_len // ts

    def seg_of(j):
        return jnp.where(j >= n_seg, 2 * n_seg - 1 - j, j)

    def seg_bwd(j):
        return jnp.where(j >= n_seg, 2 * n_seg - 1 - j, n_seg - 1)

    kq = qd // dk
    kv_ = (2 * qd) // dv
    kg = (2 * qd + vd) // dv
    return pl.pallas_call(
        functools.partial(_ret_kernel, n_seg=n_seg, ts=ts, c_len=c_len, dv=dv),
        grid=(batch, n_heads, 2 * n_seg),
        in_specs=[
            pl.BlockSpec(memory_space=pltpu.SMEM),
            pl.BlockSpec((ts, dk), lambda b, h, j: (b * n_seg + seg_of(j), h)),
            pl.BlockSpec((ts, dk), lambda b, h, j: (b * n_seg + seg_of(j), kq + h)),
            pl.BlockSpec((ts, dv), lambda b, h, j: (b * n_seg + seg_of(j), kv_ + h)),
            pl.BlockSpec((ts, dv), lambda b, h, j: (b * n_seg + seg_bwd(j), kg + h)),
            pl.BlockSpec((c_len, dk), lambda b, h, j: (b, h)),
            pl.BlockSpec((c_len, dv), lambda b, h, j: (b, qd // dv + h)),
        ],
        out_specs=pl.BlockSpec((ts, dv), lambda b, h, j: (b * n_seg + seg_bwd(j), h)),
        out_shape=jax.ShapeDtypeStruct((batch * s_len, vd), BF16),
        scratch_shapes=[pltpu.VMEM((dk, dv), F32), pltpu.VMEM((s_len, dv), F32)],
        compiler_params=_params("arbitrary", "arbitrary", "arbitrary"),
        name="retention",
    )(lg, p_lat, p_lat, p_lat, p_lat, p_ctx, p_ctx)


def _router_kernel(*refs, n_src, nl_tiles, tm, d, n_exp):
    if n_src == 2:
        hl_ref, hc_ref, wm_ref, sh_ref, w_ref, b_ref, n_ref, idx_ref, w8_ref, r8_ref, cnt_ref, carry, nhi, nlo = refs
    else:
        hl_ref, wm_ref, sh_ref, w_ref, b_ref, n_ref, idx_ref, w8_ref, r8_ref, cnt_ref, carry, nhi, nlo = refs
        hc_ref = None
    i = pl.program_id(0)
    dh = d // 2

    @pl.when(i == 0)
    def _():
        carry[...] = jnp.zeros_like(carry)

    wm = wm_ref[...]
    sh = sh_ref[...]

    def fill(src_ref):
        def body(r, c):
            rows = pl.ds(pl.multiple_of(r * NORM_ROWS, NORM_ROWS), NORM_ROWS)
            n = _norm_rows(src_ref[rows, :], wm, sh, d)
            hi = n.astype(BF16)
            hi32 = hi.astype(F32)
            bits = lax.bitcast_convert_type(hi32, jnp.uint32)
            n_ref[rows, :] = (bits[:, :dh] >> 16) | (bits[:, dh:] & jnp.uint32(0xFFFF0000))
            nhi[rows, :] = hi
            nlo[rows, :] = (n - hi32).astype(BF16)
            return c

        lax.fori_loop(0, tm // NORM_ROWS, body, 0, unroll=NORM_UNROLL)

    if n_src == 2:
        pl.when(i < nl_tiles)(lambda: fill(hl_ref))
        pl.when(i >= nl_tiles)(lambda: fill(hc_ref))
    else:
        fill(hl_ref)

    w_hi = w_ref[0:n_exp, :]
    w_lo = w_ref[n_exp:2 * n_exp, :]
    x_hi = nhi[...]
    logits = (lax.dot_general(w_hi, x_hi, _NT, preferred_element_type=F32)
              + lax.dot_general(w_lo, x_hi, _NT, preferred_element_type=F32)
              + lax.dot_general(w_hi, nlo[...], _NT, preferred_element_type=F32)
              + b_ref[...])

    eidx = lax.broadcasted_iota(I32, (n_exp, tm), 0)
    row8 = lax.broadcasted_iota(I32, (8, tm), 0)
    rest = logits
    sel = jnp.zeros((n_exp, tm), jnp.bool_)
    picks = []
    idx8 = jnp.zeros((8, tm), I32)
    top1 = None
    for k in range(TOP_K):
        mx = jnp.max(rest, axis=0, keepdims=True)
        if k == 0:
            top1 = mx
        am = jnp.min(jnp.where(rest == mx, eidx, n_exp), axis=0, keepdims=True)
        pick = eidx == am
        picks.append(pick)
        sel = sel | pick
        idx8 = jnp.where(row8 == k, am, idx8)
        rest = jnp.where(pick, -jnp.inf, rest)
    e = jnp.where(sel, jnp.exp(logits - top1), 0.0)
    gates = e / jnp.sum(e, axis=0, keepdims=True)

    t0 = lax.broadcasted_iota(I32, (tm, tm), 0)
    t1 = lax.broadcasted_iota(I32, (tm, tm), 1)
    before = jnp.where(t0 < t1, 1.0, 0.0).astype(BF16)
    self_f = jnp.where(sel, 1.0, 0.0)
    rank = jnp.dot(self_f.astype(BF16), before, preferred_element_type=F32) + carry[:, 0:1]
    w8 = jnp.zeros((8, tm), F32)
    r8 = jnp.zeros((8, tm), F32)
    for k, pick in enumerate(picks):
        w8 = jnp.where(row8 == k, jnp.sum(jnp.where(pick, gates, 0.0), axis=0, keepdims=True), w8)
        r8 = jnp.where(row8 == k, jnp.sum(jnp.where(pick, rank, 0.0), axis=0, keepdims=True), r8)
    idx_ref[...] = idx8
    w8_ref[...] = w8
    r8_ref[...] = r8.astype(I32)
    new_carry = carry[...] + jnp.sum(self_f, axis=1, keepdims=True)
    carry[...] = new_carry
    cnt_ref[...] = new_carry


def _router(h_lat, h_ctx, wm, sh, rw, rb, *, group_rows, tm):
    ml, d = h_lat.shape
    n_exp = rw.shape[1]
    mc = 0 if h_ctx is None else h_ctx.shape[0]
    t = ml + mc
    assert ml % tm == 0 and mc % tm == 0 and group_rows % tm == 0
    nl_tiles = ml // tm
    n_groups_lat = ml // group_rows
    rw_t = rw.T
    w_hi = rw_t.astype(BF16)
    w_lo = (rw_t - w_hi.astype(F32)).astype(BF16)
    w3 = jnp.concatenate([w_hi, w_lo], axis=0)

    def grp(i):
        return jnp.where(i < nl_tiles, (i * tm) // group_rows, n_groups_lat)

    in_specs = [pl.BlockSpec((tm, d), lambda i: (jnp.minimum(i, nl_tiles - 1), 0))]
    args = [h_lat]
    if mc:
        in_specs.append(pl.BlockSpec((tm, d), lambda i: (jnp.maximum(i - nl_tiles, 0), 0)))
        args.append(h_ctx)
    in_specs += [
        pl.BlockSpec((None, 1, d), lambda i: (grp(i), 0, 0)),
        pl.BlockSpec((None, 1, d), lambda i: (grp(i), 0, 0)),
        pl.BlockSpec((2 * n_exp, d), lambda i: (0, 0)),
        pl.BlockSpec((n_exp, 1), lambda i: (0, 0)),
    ]
    args += [wm, sh, w3, rb.reshape(n_exp, 1)]
    return pl.pallas_call(
        functools.partial(_router_kernel, n_src=2 if mc else 1, nl_tiles=nl_tiles, tm=tm, d=d, n_exp=n_exp),
        grid=(t // tm,),
        in_specs=in_specs,
        out_specs=[
            pl.BlockSpec((tm, d // 2), lambda i: (i, 0)),
            pl.BlockSpec((8, tm), lambda i: (0, i)),
            pl.BlockSpec((8, tm), lambda i: (0, i)),
            pl.BlockSpec((8, tm), lambda i: (0, i)),
            pl.BlockSpec((n_exp, LANES), lambda i: (0, 0)),
        ],
        out_shape=[
            jax.ShapeDtypeStruct((t, d // 2), jnp.uint32),
            jax.ShapeDtypeStruct((8, t), I32),
            jax.ShapeDtypeStruct((8, t), F32),
            jax.ShapeDtypeStruct((8, t), I32),
            jax.ShapeDtypeStruct((n_exp, LANES), F32),
        ],
        scratch_shapes=[pltpu.VMEM((n_exp, LANES), F32), pltpu.VMEM((tm, d), BF16), pltpu.VMEM((tm, d), BF16)],
        compiler_params=_params("arbitrary"),
        name="moe_router",
    )(*args)


def _routing_tables(idx8, r8, cnt, *, tm, n_tiles):
    n_exp = cnt.shape[0]
    c = cnt[:, 0].astype(I32)
    pc = ((c + tm - 1) // tm) * tm
    cum = jnp.cumsum(pc)
    off = cum - pc
    idx4 = idx8[:TOP_K]
    off4 = jnp.sum(jnp.where(idx4[:, :, None] == jnp.arange(n_exp, dtype=I32), off, 0), axis=-1)
    pos4 = (off4 + r8[:TOP_K]).T.reshape(-1)
    extents = jnp.concatenate([
        jnp.stack([cum[-1] // tm, jnp.asarray(n_tiles * tm, I32), jnp.asarray(idx8.shape[1], I32)]),
        off + c, cum]).astype(I32)
    tile_start = jnp.arange(n_tiles, dtype=I32) * tm
    tile_expert = jnp.minimum(jnp.sum(tile_start[:, None] >= cum[None, :], axis=1), n_exp - 1).astype(I32)
    return tile_expert, extents, pos4


def _moe_kernel(te_ref, nu_ref, pos_ref, n_hbm, w1_ref, b1_ref, w2_ref, b2_ref, o_ref, tok_ref, xbuf, sem, *,
                tm, f_dim, dh, n_exp):
    i = pl.program_id(0)
    n_used = nu_ref[0]
    slot = i % GATHER_BUFS

    def row_copy(tile, r, sl):
        tok = tok_ref[tile * tm + r]
        return pltpu.make_async_copy(n_hbm.at[pl.ds(tok, 1), :], xbuf.at[sl, pl.ds(r, 1), :], sem.at[sl])

    def wait_rows(sl):
        pltpu.make_async_copy(n_hbm.at[pl.ds(0, tm), :], xbuf.at[sl], sem.at[sl]).wait()

    def start_rows(tile, sl):
        def body(rb, c):
            for u in range(DMA_UNROLL):
                row_copy(tile, rb * DMA_UNROLL + u, sl).start()
            return c

        lax.fori_loop(0, tm // DMA_UNROLL, body, 0)

    @pl.when(i == 0)
    def _():
        def zero(s, c):
            tok_ref[s] = 0
            return c

        for e in range(n_exp):
            lax.fori_loop(nu_ref[3 + e], nu_ref[3 + n_exp + e], zero, 0)

        def invert(tb, c):
            for u in range(INVERT_TOKENS):
                t = tb * INVERT_TOKENS + u
                for k in range(TOP_K):
                    tok_ref[pos_ref[t * TOP_K + k]] = t
            return c

        lax.fori_loop(0, nu_ref[2] // INVERT_TOKENS, invert, 0)
        for b in range(GATHER_BUFS - 1):
            start_rows(jnp.minimum(b, n_used - 1), b)

    def step(sl):
        ahead = jnp.minimum(i + GATHER_BUFS - 1, n_used - 1)
        for r in range(tm):
            row_copy(ahead, r, (sl + GATHER_BUFS - 1) % GATHER_BUFS).start()
        wait_rows(sl)
        words = xbuf[sl]
        x_lo = lax.bitcast_convert_type(words << 16, F32).astype(BF16)
        x_hi = lax.bitcast_convert_type(words & jnp.uint32(0xFFFF0000), F32).astype(BF16)
        gu = (jnp.dot(x_lo, w1_ref[0:dh, :], preferred_element_type=F32)
              + jnp.dot(x_hi, w1_ref[dh:2 * dh, :], preferred_element_type=F32) + b1_ref[...])
        gate = jnp.minimum(gu[:, :f_dim], SWIGLU_LIMIT)
        up = jnp.clip(gu[:, f_dim:], -SWIGLU_LIMIT, SWIGLU_LIMIT)
        act = (up + 1.0) * (gate * jax.nn.sigmoid(SWIGLU_ALPHA * gate))
        o_ref[...] = jnp.dot(act.astype(BF16), w2_ref[...], preferred_element_type=F32) + b2_ref[...]

        @pl.when(i == n_used - 1)
        def _():
            for ds in range(1, GATHER_BUFS):
                wait_rows((sl + ds) % GATHER_BUFS)

    for s in range(GATHER_BUFS):
        pl.when((i < n_used) & (slot == s))(functools.partial(step, s))

    @pl.when(i >= n_used)
    def _():
        o_ref[...] = jnp.zeros_like(o_ref)


def _experts(n_packed, tile_expert, n_used, pos4, w1, b1, w2, b2, *, layer, tm, n_tiles):
    n_layers, n_exp, d, f2 = w1.shape
    f_dim = f2 // 2
    dh = n_packed.shape[1]
    grid_spec = pltpu.PrefetchScalarGridSpec(
        num_scalar_prefetch=3,
        grid=(n_tiles,),
        in_specs=[
            pl.BlockSpec(memory_space=pl.ANY),
            pl.BlockSpec((None, None, d, f2), lambda i, te, nu, pos: (layer, te[i], 0, 0)),
            pl.BlockSpec((None, None, 1, f2), lambda i, te, nu, pos: (layer, te[i], 0, 0)),
            pl.BlockSpec((None, None, f_dim, d), lambda i, te, nu, pos: (layer, te[i], 0, 0)),
            pl.BlockSpec((None, None, 1, d), lambda i, te, nu, pos: (layer, te[i], 0, 0)),
        ],
        out_specs=pl.BlockSpec((tm, d), lambda i, te, nu, pos: (i, 0)),
        scratch_shapes=[pltpu.SMEM((n_tiles * tm,), I32), pltpu.VMEM((GATHER_BUFS, tm, dh), jnp.uint32),
                        pltpu.SemaphoreType.DMA((GATHER_BUFS,))],
    )
    return pl.pallas_call(
        functools.partial(_moe_kernel, tm=tm, f_dim=f_dim, dh=dh, n_exp=n_exp),
        grid_spec=grid_spec,
        out_shape=jax.ShapeDtypeStruct((n_tiles * tm, d), F32),
        compiler_params=_params("arbitrary"),
        name="moe_experts",
    )(tile_expert, n_used, pos4, n_packed, w1, b1.reshape(n_layers, n_exp, 1, f2), w2,
      b2.reshape(n_layers, n_exp, 1, d))


def _combine_kernel(*refs, tc, d, tok_base, final):
    if final:
        pos_ref, ys_hbm, h_ref, g_ref, w_ref, fw_ref, o_ref, buf, sem = refs
    else:
        pos_ref, ys_hbm, h_ref, g_ref, w_ref, o_ref, buf, sem = refs
        fw_ref = None
    i = pl.program_id(0)
    nt = pl.num_programs(0)
    slot = i % 2

    def start_rows(tile, r0, sl):
        for u in range(NORM_ROWS):
            base = (tok_base + tile * tc + r0 + u) * TOP_K
            for k in range(TOP_K):
                pltpu.make_async_copy(ys_hbm.at[pl.ds(pos_ref[base + k], 1), :],
                                      buf.at[sl, pl.ds(k * tc + r0 + u, 1), :], sem.at[sl]).start()

    def wait_rows(sl):
        pltpu.make_async_copy(ys_hbm.at[pl.ds(0, TOP_K * tc), :], buf.at[sl], sem.at[sl]).wait()

    @pl.when(i == 0)
    def _():
        def first(r, c):
            start_rows(0, r * NORM_ROWS, 0)
            return c

        lax.fori_loop(0, tc // NORM_ROWS, first, 0)

    gate = g_ref[...]
    fw = None if fw_ref is None else fw_ref[...]
    nxt = jnp.minimum(i + 1, nt - 1)

    def step(sl):
        wait_rows(sl)
        for r0 in range(0, tc, NORM_ROWS):
            start_rows(nxt, r0, 1 - sl)
            wk = w_ref[r0:r0 + NORM_ROWS, :]
            y = wk[:, 0:1] * buf[sl, r0:r0 + NORM_ROWS, :]
            for k in range(1, TOP_K):
                y = y + wk[:, k:k + 1] * buf[sl, k * tc + r0:k * tc + r0 + NORM_ROWS, :]
            out = h_ref[r0:r0 + NORM_ROWS, :] + gate * y
            if final:
                ms = jnp.sum(out * out, axis=-1, keepdims=True) * (1.0 / d)
                out = out * lax.rsqrt(ms + NORM_EPS) * fw
            o_ref[r0:r0 + NORM_ROWS, :] = out

        @pl.when(i == nt - 1)
        def _():
            wait_rows(1 - sl)

    pl.when(slot == 0)(lambda: step(0))
    pl.when(slot == 1)(lambda: step(1))


def _combine(ys, pos4, w4, h, gate, *, group_rows, tok_base, final_w=None):
    m, d = h.shape
    tc = _tile(m, 128)
    assert group_rows % tc == 0 and tok_base % tc == 0
    final = final_w is not None
    in_specs = [
        pl.BlockSpec(memory_space=pl.ANY),
        pl.BlockSpec((tc, d), lambda i, pos: (i, 0)),
        pl.BlockSpec((None, 1, d), lambda i, pos: ((i * tc) // group_rows, 0, 0)),
        pl.BlockSpec((tc, TOP_K), lambda i, pos: (tok_base // tc + i, 0)),
    ]
    args = [pos4, ys, h, gate, w4]
    if final:
        in_specs.append(pl.BlockSpec((1, d), lambda i, pos: (0, 0)))
        args.append(final_w.reshape(1, d))
    grid_spec = pltpu.PrefetchScalarGridSpec(
        num_scalar_prefetch=1,
        grid=(m // tc,),
        in_specs=in_specs,
        out_specs=pl.BlockSpec((tc, d), lambda i, pos: (i, 0)),
        scratch_shapes=[pltpu.VMEM((2, TOP_K * tc, d), F32), pltpu.SemaphoreType.DMA((2,))],
    )
    return pl.pallas_call(
        functools.partial(_combine_kernel, tc=tc, d=d, tok_base=tok_base, final=final),
        grid_spec=grid_spec,
        out_shape=jax.ShapeDtypeStruct((m, d), F32),
        compiler_params=_params("arbitrary"),
        name="moe_combine",
    )(*args)


def _moe(h_lat, h_ctx, wm, sh, gate, rw, rb, w1, b1, w2, b2, *, layer, s_len, final_w=None):
    ml = h_lat.shape[0]
    mc = 0 if h_ctx is None else h_ctx.shape[0]
    t = ml + mc
    n_exp = rw.shape[1]
    nb = ml // s_len
    tm_r = _tile(math.gcd(ml, mc) if mc else ml, 256)
    n_packed, idx8, w8, r8, cnt = _router(h_lat, h_ctx, wm, sh, rw, rb, group_rows=s_len, tm=tm_r)
    tm = 256
    assert t % INVERT_TOKENS == 0
    n_tiles = (TOP_K * t) // tm + n_exp
    tile_expert, n_used, pos4 = _routing_tables(idx8, r8, cnt, tm=tm, n_tiles=n_tiles)
    w4 = w8[:TOP_K].T
    ys = _experts(n_packed, tile_expert, n_used, pos4, w1, b1, w2, b2, layer=layer, tm=tm, n_tiles=n_tiles)
    new_lat = _combine(ys, pos4, w4, h_lat, gate[:nb], group_rows=s_len, tok_base=0, final_w=final_w)
    new_ctx = None
    if mc:
        new_ctx = _combine(ys, pos4, w4, h_ctx, gate[nb:nb + 1], group_rows=mc, tok_base=ml)
    return new_lat, new_ctx


def _axial_tables(s_len, hd):
    half = hd // 2
    inv = 1.0 / (ROPE_BASE ** (jnp.arange(0, half, 2, dtype=F32) / half))
    t = jnp.arange(s_len)
    rows = (t // GRID_W).astype(F32)[:, None] * inv[None, :]
    cols = (t % GRID_W).astype(F32)[:, None] * inv[None, :]
    cos = jnp.concatenate([jnp.cos(rows), jnp.cos(rows), jnp.cos(cols), jnp.cos(cols)], axis=-1)
    sin = jnp.concatenate([-jnp.sin(rows), jnp.sin(rows), -jnp.sin(cols), jnp.sin(cols)], axis=-1)
    return cos, sin


def _retention_tables(s_len, dk):
    theta = 1.0 / (ROPE_BASE ** jnp.linspace(0.0, 1.0, dk // 2, dtype=F32))
    ang = jnp.arange(s_len, dtype=F32)[:, None] * theta[None, :]
    return jnp.cos(ang), jnp.sin(ang)


def kernel(x, c, ctx, c_ctx, mod_w, mod_b, norm1_w, norm2_w, attn_w_qkv, attn_w_o, attn_sinks, ret_w_in,
           ret_w_out, ret_a_fwd, ret_a_bwd, router_w, router_b, exp_w1, exp_b1, exp_w2, exp_b2, final_norm_w):
    batch, s_len, d = x.shape
    c_len = ctx.shape[1]
    depth = mod_w.shape[0]
    assert depth == 2 and attn_w_qkv.shape[0] == 1 and ret_w_in.shape[0] == 1
    n_heads = attn_sinks.shape[1]
    hd = d // n_heads
    n_kv = (attn_w_qkv.shape[2] // hd - n_heads) // 2
    r_heads = ret_a_fwd.shape[1]
    dk = d // r_heads
    dv = 2 * dk
    qd, vd = r_heads * dk, r_heads * dv
    assert hd == LANES and dk == 2 * LANES
    ml, mc = batch * s_len, batch * c_len

    n_rows = -(-(batch + 1) // 8) * 8
    cvec = jnp.zeros((n_rows, d), F32).at[:batch].set(c).at[batch].set(c_ctx)
    mods = _modulation(cvec, mod_w, mod_b)

    def mod_tables(i):
        sh1, sc1, g1, sh2, sc2, g2 = [a[:batch + 1, None, :] for a in jnp.split(mods[i], 6, axis=-1)]
        wm1 = norm1_w[i][None, None, :] * (1.0 + sc1)
        wm2 = norm2_w[i][None, None, :] * (1.0 + sc2)
        return wm1, sh1, g1, wm2, sh2, g2

    x2 = x.reshape(ml, d)
    ctx2 = ctx.reshape(mc, d)

    wm1, sh1, g1, wm2, sh2, g2 = mod_tables(0)
    w_qkv = attn_w_qkv[0].astype(BF16)
    nq = n_heads * hd
    qkv_lat = _norm_matmul(x2, wm1[:batch], sh1[:batch], w_qkv, group_rows=s_len, out_dtype=BF16,
                           rot="axial", tabs=_axial_tables(s_len, hd), rot_cols=nq + n_kv * hd,
                           name="qkv_latent")
    qkv_ctx = _norm_matmul(ctx2, wm1[batch:], sh1[batch:], w_qkv, group_rows=mc, out_dtype=BF16,
                           name="qkv_context")
    o_lat, o_ctx = _attention(qkv_lat, qkv_ctx, attn_sinks[0], batch=batch, s_len=s_len, c_len=c_len,
                              n_heads=n_heads, n_kv=n_kv, hd=hd)
    w_o = attn_w_o[0].astype(BF16)
    h_lat = _matmul_residual(o_lat, w_o, x2, g1[:batch], group_rows=s_len, name="attn_out_latent")
    h_ctx = _matmul_residual(o_ctx, w_o, ctx2, g1[batch:], group_rows=mc, name="attn_out_context")
    w1_all = exp_w1.astype(BF16)
    w2_all = exp_w2.astype(BF16)
    h_lat, h_ctx = _moe(h_lat, h_ctx, wm2, sh2, g2, router_w[0], router_b[0], w1_all, exp_b1, w2_all, exp_b2,
                        layer=0, s_len=s_len)

    wm1, sh1, g1, wm2, sh2, g2 = mod_tables(1)
    w_in = ret_w_in[0].astype(BF16)
    k_scale = dk ** -0.5
    p_lat = _norm_matmul(h_lat, wm1[:batch], sh1[:batch], w_in, group_rows=s_len, out_dtype=F32,
                         rot="half", tabs=_retention_tables(s_len, dk), rot_cols=2 * qd,
                         k_lo=qd, k_hi=2 * qd, k_scale=k_scale, name="ret_in_latent")
    p_ctx = _norm_matmul(h_ctx, wm1[batch:], sh1[batch:], w_in, group_rows=mc, out_dtype=F32,
                         col_off=qd, n_out=qd + vd, k_lo=qd, k_hi=2 * qd, k_scale=k_scale,
                         name="ret_in_context")
    lg = jnp.stack([jnp.log1p(-jnp.exp(ret_a_fwd[0].astype(F32))), jnp.log1p(-jnp.exp(ret_a_bwd[0].astype(F32)))])
    r_lat = _retention(p_lat, p_ctx, lg, batch=batch, s_len=s_len, c_len=c_len, n_heads=r_heads, dk=dk, dv=dv)
    h_lat = _matmul_residual(r_lat, ret_w_out[0].astype(BF16), h_lat, g1[:batch], group_rows=s_len,
                             name="ret_out_latent")
    out, _ = _moe(h_lat, None, wm2, sh2, g2, router_w[1], router_b[1], w1_all, exp_b1, w2_all, exp_b2,
                  layer=1, s_len=s_len, final_w=final_norm_w)
    return out.reshape(batch, s_len, d)
```
